```python
import jax
import jax.numpy as jnp
from jax import lax
import numpy as np

D_MODEL = 1024
BATCH = 16
SEQ = 4096
DEPTH = 1
DEC_BATCH = 8
DEC_SEQ = 32
PAST_LEN = 2048

CHUNK = 64
HG_HEADS = 4
HG_DK = 128
HG_DV = 128
HG_QK_WIDTH = HG_HEADS * HG_DK
HG_WIDTH = HG_HEADS * HG_DV
ML_HEADS = 4
ML_DK = 128
ML_DV = 128
ML_QK_WIDTH = ML_HEADS * ML_DK
ML_WIDTH = ML_HEADS * ML_DV
XA_HEADS = 4
XA_DH = 128
XA_WIDTH = XA_HEADS * XA_DH
N_MEM = 256
N_BRANCH = 3
BRANCH_WIDTH = 512
D_FF = 2816
CONV_W = 3
NORM_EPS = 1e-6
IN_SPLITS = (HG_QK_WIDTH, HG_QK_WIDTH, HG_WIDTH, HG_WIDTH,
             ML_QK_WIDTH, ML_QK_WIDTH, ML_WIDTH, ML_WIDTH, ML_HEADS, ML_HEADS,
             XA_WIDTH, N_BRANCH * D_MODEL)
N_IN = (2 * HG_QK_WIDTH + 2 * HG_WIDTH + 2 * ML_QK_WIDTH + 2 * ML_WIDTH
        + 2 * ML_HEADS + XA_WIDTH + N_BRANCH * D_MODEL)

kernel_name = 'hgrn2_mlstm_memxattn_convffn_stream_step'


def rms_norm(x, g):
    xf = x.astype(jnp.float32)
    y = xf * lax.rsqrt(jnp.mean(xf * xf, axis=-1, keepdims=True) + NORM_EPS)
    return (y * g.astype(jnp.float32)).astype(x.dtype)


def split_heads(x, n_heads):
    return x.reshape(x.shape[:-1] + (n_heads, x.shape[-1] // n_heads))


def block_len(T):
    return CHUNK if T % CHUNK == 0 else T


def to_blocks(a, L):
    B, T = a.shape[:2]
    a = a.astype(jnp.float32).reshape((B, T // L, L) + a.shape[2:])
    return jnp.moveaxis(jnp.moveaxis(a, 1, 0), 2, 3)


def from_blocks(a):
    NC, B, H, L, d = a.shape
    return jnp.transpose(a, (1, 0, 3, 2, 4)).reshape(B, NC * L, H, d)


def hgrn2_recurrence(q, log_f, k, v, S0):
    L = block_len(q.shape[1])
    qb, gb, kb, vb = (to_blocks(a, L) for a in (q, log_f, k, v))
    causal = jnp.tril(jnp.ones((L, L), dtype=bool))

    def step(S, blk):
        qc, gc, kc, vc = blk
        A = jnp.cumsum(gc, axis=2)
        rel = jnp.where(causal[:, :, None], A[:, :, :, None, :] - A[:, :, None, :, :], -jnp.inf)
        scores = jnp.einsum('bhtc,bhsc,bhtsc->bhts', qc, kc, jnp.exp(rel))
        o = (jnp.einsum('bhts,bhsv->bhtv', scores, vc)
             + jnp.einsum('bhtc,bhcv->bhtv', qc * jnp.exp(A), S))
        A_last = A[:, :, -1:, :]
        S_new = (jnp.exp(A_last[:, :, 0, :, None]) * S
                 + jnp.einsum('bhsc,bhsv->bhcv', kc * jnp.exp(A_last - A), vc))
        return S_new, o

    S_fin, ob = lax.scan(step, S0.astype(jnp.float32), (qb, gb, kb, vb))
    return from_blocks(ob), S_fin


def mlstm_recurrence(q, k, v, i_pre, log_f, C0, n0, m0):
    L = block_len(q.shape[1])
    qb, kb, vb = (to_blocks(a, L) for a in (q, k, v))
    ib, fb = (to_blocks(a, L) for a in (i_pre, log_f))
    causal = jnp.tril(jnp.ones((L, L), dtype=bool))

    def step(carry, blk):
        C, n, m = carry
        qc, kc, vc, ic, fc = blk
        b = jnp.cumsum(fc, axis=-1)
        log_w = jnp.where(causal, b[..., :, None] - b[..., None, :] + ic[..., None, :], -jnp.inf)
        log_inter = b + m[..., None]
        m_t = jnp.maximum(log_inter, jnp.max(log_w, axis=-1))
        w = jnp.exp(log_w - m_t[..., None])
        a = jnp.exp(log_inter - m_t)
        s = jnp.einsum('bhtd,bhsd->bhts', qc, kc) * w
        num = (jnp.einsum('bhts,bhsv->bhtv', s, vc)
               + a[..., None] * jnp.einsum('bhtd,bhdv->bhtv', qc, C))
        den = jnp.sum(s, axis=-1) + a * jnp.einsum('bhtd,bhd->bht', qc, n)
        h = num / jnp.maximum(jnp.abs(den), jnp.exp(-m_t))[..., None]
        log_last = b[..., -1:] - b + ic
        m_new = jnp.maximum(b[..., -1] + m, jnp.max(log_last, axis=-1))
        wk = jnp.exp(log_last - m_new[..., None])
        decay = jnp.exp(b[..., -1] + m - m_new)
        C_new = decay[..., None, None] * C + jnp.einsum('bhs,bhsd,bhsv->bhdv', wk, kc, vc)
        n_new = decay[..., None] * n + jnp.einsum('bhs,bhsd->bhd', wk, kc)
        return (C_new, n_new, m_new), h

    init = (C0.astype(jnp.float32), n0.astype(jnp.float32), m0.astype(jnp.float32))
    (C, n, m), hb = lax.scan(step, init, (qb, kb, vb, ib, fb))
    return from_blocks(hb), C, n, m


def memory_kv(mem, g, w_mem_kv):
    B = mem.shape[0]
    k, v = jnp.split(rms_norm(mem, g) @ w_mem_kv, 2, axis=-1)
    return (k.reshape(B, N_MEM, XA_HEADS, XA_DH), v.reshape(B, N_MEM, XA_HEADS, XA_DH))


def encoder_layer(h, s_hgrn, s_C, s_n, s_m, s_conv, mem_k, mem_v, lb,
                  norm1, w_in, b_in, ml_fgate_bias, hg_norm, ml_norm, w_branch, w_out,
                  norm2, w_up, ffn_conv_w, ffn_conv_b, w_down):
    B, T, _ = h.shape
    dt = h.dtype
    split_points = np.cumsum(IN_SPLITS)[:-1].tolist()
    u = rms_norm(h, norm1)
    z = u @ w_in + b_in
    hq, hf, hi, hg, mq, mk, mv, mo, mi, mf, xq, gl = jnp.split(z, split_points, axis=-1)

    f = lb + (1.0 - lb) * jax.nn.sigmoid(hf.astype(jnp.float32))
    o_h, s_hgrn = hgrn2_recurrence(split_heads(jax.nn.silu(hq), HG_HEADS),
                                   split_heads(jnp.log(f), HG_HEADS),
                                   split_heads(1.0 - f, HG_HEADS),
                                   split_heads(hi, HG_HEADS), s_hgrn)
    o_h = rms_norm(o_h.astype(dt), hg_norm.reshape(HG_HEADS, HG_DV)).reshape(B, T, HG_WIDTH)
    o_h = o_h * jax.nn.silu(hg)

    log_fg = jax.nn.log_sigmoid((mf + ml_fgate_bias).astype(jnp.float32))
    h_m, s_C, s_n, s_m = mlstm_recurrence(split_heads(mq, ML_HEADS),
                                          split_heads(mk, ML_HEADS) * (ML_DK ** -0.5),
                                          split_heads(mv, ML_HEADS),
                                          mi.astype(jnp.float32), log_fg, s_C, s_n, s_m)
    h_m = rms_norm(h_m.astype(dt), ml_norm.reshape(ML_HEADS, ML_DV)).reshape(B, T, ML_WIDTH)
    h_m = h_m * jax.nn.sigmoid(mo)

    q_x = split_heads(xq, XA_HEADS)
    logits = jnp.einsum('bthd,bmhd->bhtm', q_x, mem_k) * (XA_DH ** -0.5)
    p = jax.nn.softmax(logits.astype(jnp.float32), axis=-1).astype(dt)
    o_x = jnp.einsum('bhtm,bmhd->bthd', p, mem_v).reshape(B, T, XA_WIDTH)

    gates = jax.nn.sigmoid(gl.reshape(B, T, N_BRANCH, D_MODEL))
    merged = gates[:, :, 0] * (o_h @ w_branch[0])
    merged = merged + gates[:, :, 1] * (h_m @ w_branch[1])
    merged = merged + gates[:, :, 2] * (o_x @ w_branch[2])
    h = h + merged @ w_out

    up_g, up_v = jnp.split(rms_norm(h, norm2) @ w_up, 2, axis=-1)
    full = jnp.concatenate([s_conv.astype(up_g.dtype), up_g], axis=1)
    conv = ffn_conv_b + ffn_conv_w[0] * full[:, 0:T]
    for j in range(1, CONV_W):
        conv = conv + ffn_conv_w[j] * full[:, j:j + T]
    s_conv = full[:, T:]
    h = h + (jax.nn.gelu(conv) * up_v) @ w_down
    return (h, s_hgrn.astype(dt), s_C.astype(dt), s_n.astype(dt), s_m.astype(dt), s_conv)


def setup_inputs(seed: int = 0) -> dict:
    key = jax.random.key(seed)
    ks = jax.random.split(key, 32)
    f32 = jnp.float32

    def nrm(k, shape, scale):
        return scale * jax.random.normal(k, shape, f32)

    def gain(k, shape):
        return 1.0 + 0.05 * jax.random.normal(k, shape, f32)

    return {
        'x_prompt': nrm(ks[0], (BATCH, SEQ, D_MODEL), 1.0),
        'x_sample': nrm(ks[1], (DEC_BATCH, DEC_SEQ, D_MODEL), 1.0),
        'state_hgrn': nrm(ks[2], (DEPTH, DEC_BATCH, HG_HEADS, HG_DK, HG_DV), 0.5),
        'state_mlstm_C': nrm(ks[3], (DEPTH, DEC_BATCH, ML_HEADS, ML_DK, ML_DV), 0.1),
        'state_mlstm_n': nrm(ks[4], (DEPTH, DEC_BATCH, ML_HEADS, ML_DK), 0.1),
        'state_mlstm_m': nrm(ks[5], (DEPTH, DEC_BATCH, ML_HEADS), 1.0),
        'state_ffn_conv': nrm(ks[6], (DEPTH, DEC_BATCH, CONV_W - 1, D_FF), 1.0),
        'cache_mem_k': nrm(ks[7], (DEPTH, DEC_BATCH, N_MEM, XA_HEADS, XA_DH), 1.0),
        'cache_mem_v': nrm(ks[8], (DEPTH, DEC_BATCH, N_MEM, XA_HEADS, XA_DH), 1.0),
        'mem_prompt': nrm(ks[9], (BATCH, N_MEM, D_MODEL), 1.0),
        'norm1': gain(ks[10], (DEPTH, D_MODEL)),
        'w_in': nrm(ks[11], (DEPTH, D_MODEL, N_IN), D_MODEL ** -0.5),
        'b_in': nrm(ks[12], (DEPTH, N_IN), 0.02),
        'ml_fgate_bias': jnp.linspace(3.0, 6.0, ML_HEADS, dtype=f32)[None, :] + nrm(ks[13], (DEPTH, ML_HEADS), 0.1),
        'hg_lb_logits': nrm(ks[14], (DEPTH + 1, HG_QK_WIDTH), 0.5),
        'hg_norm': gain(ks[15], (DEPTH, HG_WIDTH)),
        'ml_norm': gain(ks[16], (DEPTH, ML_WIDTH)),
        'mem_norm': gain(ks[17], (DEPTH, D_MODEL)),
        'w_mem_kv': nrm(ks[18], (DEPTH, D_MODEL, 2 * XA_WIDTH), D_MODEL ** -0.5),
        'w_branch': nrm(ks[19], (DEPTH, N_BRANCH, BRANCH_WIDTH, D_MODEL), BRANCH_WIDTH ** -0.5),
        'w_out': nrm(ks[20], (DEPTH, D_MODEL, D_MODEL), D_MODEL ** -0.5),
        'norm2': gain(ks[21], (DEPTH, D_MODEL)),
        'w_up': nrm(ks[22], (DEPTH, D_MODEL, 2 * D_FF), D_MODEL ** -0.5),
        'ffn_conv_w': nrm(ks[23], (DEPTH, CONV_W, D_FF), CONV_W ** -0.5),
        'ffn_conv_b': nrm(ks[24], (DEPTH, D_FF), 0.02),
        'w_down': nrm(ks[25], (DEPTH, D_FF, D_MODEL), D_FF ** -0.5),
        'final_norm': gain(ks[26], (D_MODEL,)),
    }


def reference(x_prompt, x_sample, state_hgrn, state_mlstm_C, state_mlstm_n, state_mlstm_m,
              state_ffn_conv, cache_mem_k, cache_mem_v, mem_prompt,
              norm1, w_in, b_in, ml_fgate_bias, hg_lb_logits, hg_norm, ml_norm, mem_norm,
              w_mem_kv, w_branch, w_out, norm2, w_up, ffn_conv_w, ffn_conv_b, w_down, final_norm):
    dt = x_prompt.dtype
    Bp = x_prompt.shape[0]
    lbs = jnp.cumsum(jax.nn.softmax(hg_lb_logits.astype(jnp.float32), axis=0), axis=0)
    hp, hs = x_prompt, x_sample
    p_hg, p_C, p_n, p_m, p_mk, p_mv, p_cv = [], [], [], [], [], [], []
    s_hg, s_C, s_n, s_m, s_cv = [], [], [], [], []
    for l in range(DEPTH):
        lw = (norm1[l], w_in[l], b_in[l], ml_fgate_bias[l], hg_norm[l], ml_norm[l], w_branch[l],
              w_out[l], norm2[l], w_up[l], ffn_conv_w[l], ffn_conv_b[l], w_down[l])
        mk_p, mv_p = memory_kv(mem_prompt, mem_norm[l], w_mem_kv[l])
        hp, a1, a2, a3, a4, a5 = encoder_layer(
            hp,
            jnp.zeros((Bp, HG_HEADS, HG_DK, HG_DV), dt),
            jnp.zeros((Bp, ML_HEADS, ML_DK, ML_DV), dt),
            jnp.zeros((Bp, ML_HEADS, ML_DK), dt),
            jnp.zeros((Bp, ML_HEADS), dt),
            jnp.zeros((Bp, CONV_W - 1, D_FF), dt),
            mk_p, mv_p, lbs[l], *lw)
        p_hg.append(a1); p_C.append(a2); p_n.append(a3); p_m.append(a4); p_cv.append(a5)
        p_mk.append(mk_p); p_mv.append(mv_p)
        hs, b1, b2, b3, b4, b5 = encoder_layer(
            hs, state_hgrn[l], state_mlstm_C[l], state_mlstm_n[l], state_mlstm_m[l],
            state_ffn_conv[l], cache_mem_k[l], cache_mem_v[l], lbs[l], *lw)
        s_hg.append(b1); s_C.append(b2); s_n.append(b3); s_m.append(b4); s_cv.append(b5)
    y_prompt = rms_norm(hp, final_norm)
    y_sample = rms_norm(hs, final_norm)
    return (y_prompt, y_sample,
            jnp.stack(p_hg), jnp.stack(p_C), jnp.stack(p_n), jnp.stack(p_m),
            jnp.stack(p_mk), jnp.stack(p_mv), jnp.stack(p_cv),
            jnp.stack(s_hg), jnp.stack(s_C), jnp.stack(s_n), jnp.stack(s_m), jnp.stack(s_cv))
```

```python
import functools

import jax
import jax.numpy as jnp
from jax import lax
from jax.experimental import pallas as pl
from jax.experimental.pallas import tpu as pltpu

HEADS = 4
HEAD_DIM = 128
WIDTH = HEADS * HEAD_DIM
N_BRANCH = 3
CONV_W = 3
NORM_EPS = 1e-6
RECURRENCE_CHUNK = 64
MLSTM_CHUNK = 128
TOKEN_TILE = 256
COL_GROUP = 512
VMEM_LIMIT_BYTES = 56 * 1024 * 1024

F32 = jnp.float32
BF16 = jnp.bfloat16

_HQ, _HF, _HI, _HG = 0, WIDTH, 2 * WIDTH, 3 * WIDTH
_MQ, _MK, _MV, _MO = 4 * WIDTH, 5 * WIDTH, 6 * WIDTH, 7 * WIDTH
_XQ = 8 * WIDTH
_GL = 9 * WIDTH


def _dot(a, b):
    return jnp.dot(a, b, preferred_element_type=F32)


def _dot_nt(a, b):
    return lax.dot_general(a, b, (((1,), (1,)), ((), ())), preferred_element_type=F32)


def _dot_tn(a, b):
    return lax.dot_general(a, b, (((0,), (0,)), ((), ())), preferred_element_type=F32)


def _sigmoid(x):
    return 1.0 / (1.0 + jnp.exp(-x))


def _log_sigmoid(x):
    return jnp.minimum(x, 0.0) - jnp.log(1.0 + jnp.exp(-jnp.abs(x)))


def _rms(x, g):
    return x * lax.rsqrt(jnp.mean(x * x, axis=-1, keepdims=True) + NORM_EPS) * g


def _split_bf16(x):
    hi = x.astype(BF16)
    lo = (x - hi.astype(F32)).astype(BF16)
    return hi, lo


def _lower_tri(n):
    r = lax.broadcasted_iota(jnp.int32, (n, n), 0)
    c = lax.broadcasted_iota(jnp.int32, (n, n), 1)
    return r >= c


def _memkv_kernel(mem_ref, g_ref, w_ref, k_ref, v_ref, kb_ref, vb_ref):
    u = _rms(mem_ref[0], g_ref[...]).astype(BF16)
    kv = _dot(u, w_ref[...])
    k = kv[:, :WIDTH]
    v = kv[:, WIDTH:]
    k_ref[0] = k
    v_ref[0] = v
    kb_ref[0] = k.astype(BF16)
    vb_ref[0] = v.astype(BF16)


def _memory_kv(mem, g, w):
    B, M, D = mem.shape
    full = lambda shape: pl.BlockSpec(shape, lambda b: (0,) * len(shape))
    per_b = lambda shape: pl.BlockSpec((1,) + shape, lambda b: (b, 0, 0))
    return pl.pallas_call(
        _memkv_kernel,
        grid=(B,),
        in_specs=[per_b((M, D)), full((1, D)), full((D, 2 * WIDTH))],
        out_specs=[per_b((M, WIDTH))] * 4,
        out_shape=[jax.ShapeDtypeStruct((B, M, WIDTH), F32)] * 2
        + [jax.ShapeDtypeStruct((B, M, WIDTH), BF16)] * 2,
        compiler_params=pltpu.CompilerParams(dimension_semantics=("arbitrary",)),
        name="memory_kv",
    )(mem, g.reshape(1, D), w.astype(BF16))


def _mixer_kernel(x_ref, s0_ref, c0_ref, n0_ref, m0_ref, mk_ref, mv_ref,
                  norm1_ref, wmain_ref, bmain_ref, wg_ref, wgt_ref, bgrow_ref, bgcol_ref,
                  lbl_ref, hgn_ref, mln_ref, wbr_ref, wout_ref,
                  h_ref, s_out_ref, c_out_ref, n_out_ref, m_out_ref,
                  z_ref, br_ref, gcol_ref, grow_ref, st_ref, c_ref, n_ref, m_ref,
                  *, tile, hg_chunk, ml_chunk):
    t = pl.program_id(1)
    last_t = pl.num_programs(1) - 1

    @pl.when(t == 0)
    def _load_state():
        for h in range(HEADS):
            st_ref[h] = s0_ref[0, h].T
            c_ref[h] = c0_ref[0, h]
            n_ref[h] = n0_ref[0, h:h + 1, :]
            m_ref[h] = m0_ref[0, h:h + 1, :]

    x = x_ref[0]
    ub = _rms(x, norm1_ref[...]).astype(BF16)
    n_main = wmain_ref.shape[1]
    for c0 in range(0, n_main, COL_GROUP):
        cols = slice(c0, c0 + COL_GROUP)
        z_ref[:, cols] = _dot(ub, wmain_ref[:, cols]) + bmain_ref[:, cols]
    gcol_ref[...] = _dot(ub, wg_ref[...]) + bgrow_ref[...]
    grow_ref[...] = _dot_nt(wgt_ref[...], ub) + bgcol_ref[...]

    logits = lbl_ref[...]
    e = jnp.exp(logits - jnp.max(logits, axis=0, keepdims=True))
    lb = e[0:1, :] / jnp.sum(e, axis=0, keepdims=True)
    L = hg_chunk
    causal = _lower_tri(L)
    tri = jnp.where(causal, 1.0, 0.0).astype(BF16)
    hgn = hgn_ref[...]
    for ck in range(tile // L):
        rows = slice(ck * L, (ck + 1) * L)
        hq = z_ref[rows, _HQ:_HQ + WIDTH]
        hf = z_ref[rows, _HF:_HF + WIDTH]
        f = lb + (1.0 - lb) * _sigmoid(hf)
        g_hi, g_lo = _split_bf16(jnp.log(f))
        a_cum = _dot(tri, g_hi) + _dot(tri, g_lo)
        a_mid = a_cum[L // 2 - 1:L // 2, :]
        a_end = a_cum[L - 1:L, :]
        q_rel = hq * _sigmoid(hq) * jnp.exp(a_cum - a_mid)
        k_rel = (1.0 - f) * jnp.exp(a_mid - a_cum)
        q_in = (q_rel * jnp.exp(a_mid)).astype(BF16)
        k_out = (k_rel * jnp.exp(a_end - a_mid)).astype(BF16)
        q_rel = q_rel.astype(BF16)
        k_rel = k_rel.astype(BF16)
        v = z_ref[rows, _HI:_HI + WIDTH].astype(BF16)
        hgate = z_ref[rows, _HG:_HG + WIDTH]
        decay = jnp.exp(a_end)
        for h in range(HEADS):
            hs = slice(h * HEAD_DIM, (h + 1) * HEAD_DIM)
            scores = jnp.where(causal, _dot_nt(q_rel[:, hs], k_rel[:, hs]), 0.0)
            s_t = st_ref[h]
            o = _dot(scores.astype(BF16), v[:, hs]) + _dot_nt(q_in[:, hs], s_t.astype(BF16))
            st_ref[h] = decay[:, hs] * s_t + _dot_tn(v[:, hs], k_out[:, hs])
            o = _rms(o, hgn[:, hs])
            gate = hgate[:, hs]
            br_ref[rows, hs] = (o * (gate * _sigmoid(gate))).astype(BF16)

    L = ml_chunk
    causal = _lower_tri(L)
    tri = jnp.where(causal, 1.0, 0.0).astype(BF16)
    tri_t = jnp.where(lax.broadcasted_iota(jnp.int32, (L, L), 0) <= lax.broadcasted_iota(jnp.int32, (L, L), 1),
                      1.0, 0.0).astype(BF16)
    lane = lax.broadcasted_iota(jnp.int32, (L, HEAD_DIM), 1)
    sub = lax.broadcasted_iota(jnp.int32, (grow_ref.shape[0], L), 0)
    mln = mln_ref[...]
    for ck in range(tile // L):
        rows = slice(ck * L, (ck + 1) * L)
        g_c = gcol_ref[rows, :]
        g_r = grow_ref[:, rows]
        f_c = jnp.where(lane >= HEADS, _log_sigmoid(g_c), 0.0)
        f_r = jnp.where(sub >= HEADS, _log_sigmoid(g_r), 0.0)
        fc_hi, fc_lo = _split_bf16(f_c)
        fr_hi, fr_lo = _split_bf16(f_r)
        b_c = _dot(tri, fc_hi) + _dot(tri, fc_lo)
        b_r = _dot(fr_hi, tri_t) + _dot(fr_lo, tri_t)
        for h in range(HEADS):
            hs = slice(h * HEAD_DIM, (h + 1) * HEAD_DIM)
            i_col = g_c[:, h:h + 1]
            i_row = g_r[h:h + 1, :]
            b_col = b_c[:, HEADS + h:HEADS + h + 1]
            b_row = b_r[HEADS + h:HEADS + h + 1, :]
            m_prev = m_ref[h][:, 0:1]
            log_w = jnp.where(causal, b_col - b_row + i_row, -jnp.inf)
            log_inter = b_col + m_prev
            m_t = jnp.maximum(log_inter, jnp.max(log_w, axis=-1, keepdims=True))
            w = jnp.exp(log_w - m_t)
            a = jnp.exp(log_inter - m_t)
            q = z_ref[rows, _MQ + h * HEAD_DIM:_MQ + (h + 1) * HEAD_DIM]
            k = z_ref[rows, _MK + h * HEAD_DIM:_MK + (h + 1) * HEAD_DIM] * (HEAD_DIM ** -0.5)
            vb = z_ref[rows, _MV + h * HEAD_DIM:_MV + (h + 1) * HEAD_DIM].astype(BF16)
            qb = q.astype(BF16)
            s = _dot_nt(qb, k.astype(BF16)) * w
            c_prev = c_ref[h]
            n_prev = n_ref[h]
            num = _dot(s.astype(BF16), vb) + a * _dot(qb, c_prev.astype(BF16))
            den = jnp.sum(s, axis=-1, keepdims=True) + a * jnp.sum(q * n_prev, axis=-1, keepdims=True)
            hval = num / jnp.maximum(jnp.abs(den), jnp.exp(-m_t))
            b_last = b_row[:, L - 1:L]
            m_new = jnp.maximum(b_last + m_prev,
                                jnp.max(b_last - b_row + i_row, axis=-1, keepdims=True))
            kw = k * jnp.exp(b_last - b_col + i_col - m_new)
            dec = jnp.exp(b_last + m_prev - m_new)
            c_ref[h] = dec * c_prev + _dot_tn(kw.astype(BF16), vb)
            n_ref[h] = dec * n_prev + jnp.sum(kw, axis=0, keepdims=True)
            m_ref[h] = jnp.broadcast_to(m_new, (1, HEAD_DIM))
            hval = _rms(hval, mln[:, hs])
            og = z_ref[rows, _MO + h * HEAD_DIM:_MO + (h + 1) * HEAD_DIM]
            br_ref[rows, WIDTH + h * HEAD_DIM:WIDTH + (h + 1) * HEAD_DIM] = (hval * _sigmoid(og)).astype(BF16)

    for h in range(HEADS):
        hs = slice(h * HEAD_DIM, (h + 1) * HEAD_DIM)
        qx = z_ref[:, _XQ + h * HEAD_DIM:_XQ + (h + 1) * HEAD_DIM].astype(BF16)
        lg = _dot_nt(qx, mk_ref[0, :, hs]) * (HEAD_DIM ** -0.5)
        p = jnp.exp(lg - jnp.max(lg, axis=-1, keepdims=True))
        p = p / jnp.sum(p, axis=-1, keepdims=True)
        br_ref[:, 2 * WIDTH + h * HEAD_DIM:2 * WIDTH + (h + 1) * HEAD_DIM] = _dot(
            p.astype(BF16), mv_ref[0, :, hs]).astype(BF16)

    d_model = x.shape[-1]
    merged = None
    for i in range(N_BRANCH):
        y = _dot(br_ref[:, i * WIDTH:(i + 1) * WIDTH], wbr_ref[i])
        y = _sigmoid(z_ref[:, _GL + i * d_model:_GL + (i + 1) * d_model]) * y
        merged = y if merged is None else merged + y
    h_ref[0] = x + _dot(merged.astype(BF16), wout_ref[...])

    @pl.when(t == last_t)
    def _store_state():
        for h in range(HEADS):
            s_out_ref[0, h] = st_ref[h].T
            c_out_ref[0, h] = c_ref[h]
            n_out_ref[0, h:h + 1, :] = n_ref[h]
            m_out_ref[0, h:h + 1, :] = m_ref[h]


def _mixer(x, s0, c0, n0, m0, mk, mv, p):
    B, T, D = x.shape
    tile = TOKEN_TILE if T % TOKEN_TILE == 0 else T
    hg_chunk = RECURRENCE_CHUNK if tile % RECURRENCE_CHUNK == 0 else tile
    ml_chunk = MLSTM_CHUNK if tile % MLSTM_CHUNK == 0 else tile
    n_main = p["w_main"].shape[1]
    n_slots = p["lb_logits"].shape[0]
    M = mk.shape[1]
    gate_rows = p["w_gate_t"].shape[0]

    def full(shape):
        return pl.BlockSpec(shape, lambda b, t: (0,) * len(shape), pipeline_mode=pl.Buffered(1))

    def per_b(shape):
        return pl.BlockSpec((1,) + shape, lambda b, t: (b,) + (0,) * len(shape))

    state = per_b((HEADS, HEAD_DIM, HEAD_DIM))
    vec = per_b((HEADS, HEAD_DIM))
    m0_rep = jnp.broadcast_to(m0[:, :, None], (B, HEADS, HEAD_DIM))
    kern = functools.partial(_mixer_kernel, tile=tile, hg_chunk=hg_chunk, ml_chunk=ml_chunk)
    return pl.pallas_call(
        kern,
        grid=(B, T // tile),
        in_specs=[
            pl.BlockSpec((1, tile, D), lambda b, t: (b, t, 0)),
            state, state, vec, vec, per_b((M, WIDTH)), per_b((M, WIDTH)),
            full((1, D)), full((D, n_main)), full((1, n_main)),
            full((D, HEAD_DIM)), full((gate_rows, D)), full((1, HEAD_DIM)), full((gate_rows, 1)),
            full((n_slots, WIDTH)), full((1, WIDTH)), full((1, WIDTH)),
            full((N_BRANCH, WIDTH, D)), full((D, D)),
        ],
        out_specs=[pl.BlockSpec((1, tile, D), lambda b, t: (b, t, 0)), state, state, vec, vec],
        out_shape=[
            jax.ShapeDtypeStruct((B, T, D), F32),
            jax.ShapeDtypeStruct((B, HEADS, HEAD_DIM, HEAD_DIM), F32),
            jax.ShapeDtypeStruct((B, HEADS, HEAD_DIM, HEAD_DIM), F32),
            jax.ShapeDtypeStruct((B, HEADS, HEAD_DIM), F32),
            jax.ShapeDtypeStruct((B, HEADS, HEAD_DIM), F32),
        ],
        scratch_shapes=[
            pltpu.VMEM((tile, n_main), F32),
            pltpu.VMEM((tile, N_BRANCH * WIDTH), BF16),
            pltpu.VMEM((tile, HEAD_DIM), F32),
            pltpu.VMEM((gate_rows, tile), F32),
            pltpu.VMEM((HEADS, HEAD_DIM, HEAD_DIM), F32),
            pltpu.VMEM((HEADS, HEAD_DIM, HEAD_DIM), F32),
            pltpu.VMEM((HEADS, 1, HEAD_DIM), F32),
            pltpu.VMEM((HEADS, 1, HEAD_DIM), F32),
        ],
        compiler_params=pltpu.CompilerParams(
            dimension_semantics=("arbitrary", "arbitrary"), vmem_limit_bytes=VMEM_LIMIT_BYTES),
        name="mixer",
    )(x, s0, c0, n0, m0_rep, mk, mv,
      p["norm1"], p["w_main"], p["b_main"], p["w_gate"], p["w_gate_t"], p["b_gate_row"], p["b_gate_col"],
      p["lb_logits"], p["hg_norm"], p["ml_norm"], p["w_branch"], p["w_out"])


def _gelu_tanh(x):
    return 0.5 * x * (1.0 + jnp.tanh(0.7978845608028654 * (x + 0.044715 * (x * x * x))))


def _ffn_kernel(h_ref, cv0_ref, norm2_ref, wup_ref, cw_ref, cb_ref, wdn_ref, fn_ref,
                y_ref, cv_out_ref, ug_ref, act_ref, *, tile, d_ff):
    t = pl.program_id(1)
    last_t = pl.num_programs(1) - 1
    head = 8
    keep = CONV_W - 1

    @pl.when(t == 0)
    def _first():
        ug_ref[head - keep:head, :] = cv0_ref[0]

    @pl.when(t > 0)
    def _carry():
        ug_ref[head - keep:head, :] = ug_ref[head + tile - keep:head + tile, :]

    h = h_ref[0]
    hb = _rms(h, norm2_ref[...]).astype(BF16)
    for c0 in range(0, d_ff, 256):
        cols = slice(c0, c0 + 256)
        ug_ref[head:head + tile, cols] = _dot(hb, wup_ref[:, cols])
    for c0 in range(0, d_ff, 256):
        cols = slice(c0, c0 + 256)
        up_v = _dot(hb, wup_ref[:, d_ff + c0:d_ff + c0 + 256])
        conv = cb_ref[:, cols] + cw_ref[0:1, cols] * ug_ref[head - 2:head - 2 + tile, cols]
        for j in range(1, CONV_W):
            conv = conv + cw_ref[j:j + 1, cols] * ug_ref[head - 2 + j:head - 2 + j + tile, cols]
        act_ref[:, cols] = (_gelu_tanh(conv) * up_v).astype(BF16)
    h2 = h + _dot(act_ref[...], wdn_ref[...])
    y_ref[0] = _rms(h2, fn_ref[...])

    @pl.when(t == last_t)
    def _store():
        cv_out_ref[0] = ug_ref[head + tile - keep:head + tile, :]


def _ffn(h, cv0, p):
    B, T, D = h.shape
    d_ff = p["w_down"].shape[0]
    tile = TOKEN_TILE if T % TOKEN_TILE == 0 else T
    keep = CONV_W - 1

    def full(shape):
        return pl.BlockSpec(shape, lambda b, t: (0,) * len(shape), pipeline_mode=pl.Buffered(1))

    kern = functools.partial(_ffn_kernel, tile=tile, d_ff=d_ff)
    return pl.pallas_call(
        kern,
        grid=(B, T // tile),
        in_specs=[
            pl.BlockSpec((1, tile, D), lambda b, t: (b, t, 0)),
            pl.BlockSpec((1, keep, d_ff), lambda b, t: (b, 0, 0)),
            full((1, D)), full((D, 2 * d_ff)), full((CONV_W, d_ff)), full((1, d_ff)),
            full((d_ff, D)), full((1, D)),
        ],
        out_specs=[pl.BlockSpec((1, tile, D), lambda b, t: (b, t, 0)),
                   pl.BlockSpec((1, keep, d_ff), lambda b, t: (b, 0, 0))],
        out_shape=[jax.ShapeDtypeStruct((B, T, D), F32),
                   jax.ShapeDtypeStruct((B, keep, d_ff), F32)],
        scratch_shapes=[pltpu.VMEM((8 + tile, d_ff), F32),
                        pltpu.VMEM((tile, d_ff), BF16)],
        compiler_params=pltpu.CompilerParams(
            dimension_semantics=("arbitrary", "arbitrary"), vmem_limit_bytes=VMEM_LIMIT_BYTES),
        name="conv_ffn",
    )(h, cv0, p["norm2"], p["w_up"], p["conv_w"], p["conv_b"], p["w_down"], p["final_norm"])


def _prepare(norm1, w_in, b_in, ml_fgate_bias, hg_lb_logits, hg_norm, ml_norm, w_branch, w_out,
             norm2, w_up, ffn_conv_w, ffn_conv_b, w_down, final_norm):
    D = w_in.shape[0]
    g0 = 8 * WIDTH
    g1 = g0 + 2 * HEADS
    w_main = jnp.concatenate([w_in[:, :g0], w_in[:, g1:]], axis=1).astype(BF16)
    b_main = jnp.concatenate([b_in[:g0], b_in[g1:]])[None, :]
    w_gate = jnp.zeros((D, HEAD_DIM), F32).at[:, :2 * HEADS].set(w_in[:, g0:g1]).astype(BF16)
    gate_rows = 16
    w_gate_t = jnp.zeros((gate_rows, D), F32).at[:2 * HEADS].set(w_in[:, g0:g1].T).astype(BF16)
    b_gate = b_in[g0:g1] + jnp.concatenate([jnp.zeros((HEADS,), F32), ml_fgate_bias])
    return dict(
        norm1=norm1[None, :], w_main=w_main, b_main=b_main, w_gate=w_gate, w_gate_t=w_gate_t,
        b_gate_row=jnp.zeros((1, HEAD_DIM), F32).at[0, :2 * HEADS].set(b_gate),
        b_gate_col=jnp.zeros((gate_rows, 1), F32).at[:2 * HEADS, 0].set(b_gate),
        lb_logits=hg_lb_logits, hg_norm=hg_norm[None, :], ml_norm=ml_norm[None, :],
        w_branch=w_branch.astype(BF16), w_out=w_out.astype(BF16),
        norm2=norm2[None, :], w_up=w_up.astype(BF16), conv_w=ffn_conv_w, conv_b=ffn_conv_b[None, :],
        w_down=w_down.astype(BF16), final_norm=final_norm[None, :])


def _layer(x, s_hg, s_c, s_n, s_m, s_cv, mk, mv, p):
    h1, hg, c, n, m = _mixer(x, s_hg, s_c, s_n, s_m, mk, mv, p)
    y, cv = _ffn(h1, s_cv, p)
    return y, hg, c, n, m[:, :, 0], cv


def kernel(x_prompt, x_sample, state_hgrn, state_mlstm_C, state_mlstm_n, state_mlstm_m, state_ffn_conv, cache_mem_k, cache_mem_v, mem_prompt, norm1, w_in, b_in, ml_fgate_bias, hg_lb_logits, hg_norm, ml_norm, mem_norm, w_mem_kv, w_branch, w_out, norm2, w_up, ffn_conv_w, ffn_conv_b, w_down, final_norm):
    depth = norm1.shape[0]
    assert depth == 1, "single-layer encoder"
    assert hg_lb_logits.shape[0] == depth + 1
    Bp = x_prompt.shape[0]
    Bs = x_sample.shape[0]
    M = mem_prompt.shape[1]
    d_ff = w_down.shape[1]
    l = 0
    p = _prepare(norm1[l], w_in[l], b_in[l], ml_fgate_bias[l], hg_lb_logits, hg_norm[l], ml_norm[l],
                 w_branch[l], w_out[l], norm2[l], w_up[l], ffn_conv_w[l], ffn_conv_b[l], w_down[l], final_norm)

    mk_p, mv_p, mk_pb, mv_pb = _memory_kv(mem_prompt, mem_norm[l], w_mem_kv[l])
    zeros = lambda *s: jnp.zeros(s, F32)
    yp, hg_p, c_p, n_p, m_p, cv_p = _layer(
        x_prompt, zeros(Bp, HEADS, HEAD_DIM, HEAD_DIM), zeros(Bp, HEADS, HEAD_DIM, HEAD_DIM),
        zeros(Bp, HEADS, HEAD_DIM), zeros(Bp, HEADS), zeros(Bp, CONV_W - 1, d_ff), mk_pb, mv_pb, p)
    ys, hg_s, c_s, n_s, m_s, cv_s = _layer(
        x_sample, state_hgrn[l], state_mlstm_C[l], state_mlstm_n[l], state_mlstm_m[l], state_ffn_conv[l],
        cache_mem_k[l].reshape(Bs, M, WIDTH).astype(BF16), cache_mem_v[l].reshape(Bs, M, WIDTH).astype(BF16), p)

    mem_shape = (1, Bp, M, HEADS, HEAD_DIM)
    return (yp, ys,
            hg_p[None], c_p[None], n_p[None], m_p[None],
            mk_p.reshape(mem_shape), mv_p.reshape(mem_shape), cv_p[None],
            hg_s[None], c_s[None], n_s[None], m_s[None], cv_s[None])
```

```python
import functools

import jax
import jax.numpy as jnp
from jax import lax
from jax.experimental import pallas as pl
from jax.experimental.pallas import tpu as pltpu

HEADS = 4
HEAD_DIM = 128
WIDTH = HEADS * HEAD_DIM
N_BRANCH = 3
CONV_W = 3
NORM_EPS = 1e-6
RECURRENCE_CHUNK = 64
MLSTM_CHUNK = 128
TOKEN_TILE = 256
COL_GROUP = 512
VMEM_LIMIT_BYTES = 56 * 1024 * 1024

F32 = jnp.float32
BF16 = jnp.bfloat16

_HQ, _HF, _HI, _HG = 0, WIDTH, 2 * WIDTH, 3 * WIDTH
_MQ, _MK, _MV, _MO = 4 * WIDTH, 5 * WIDTH, 6 * WIDTH, 7 * WIDTH
_XQ = 8 * WIDTH
_GL = 9 * WIDTH

_HEAD_COLS = tuple(slice(h * HEAD_DIM, (h + 1) * HEAD_DIM) for h in range(HEADS))


def _dot(a, b):
    return jnp.dot(a, b, preferred_element_type=F32)


def _dot_nt(a, b):
    return lax.dot_general(a, b, (((1,), (1,)), ((), ())), preferred_element_type=F32)


def _dot_tn(a, b):
    return lax.dot_general(a, b, (((0,), (0,)), ((), ())), preferred_element_type=F32)


def _sigmoid(x):
    return 1.0 / (1.0 + jnp.exp(-x))


def _log_sigmoid(x):
    return jnp.minimum(x, 0.0) - jnp.log(1.0 + jnp.exp(-jnp.abs(x)))


def _rms(x, g):
    return x * lax.rsqrt(jnp.mean(x * x, axis=-1, keepdims=True) + NORM_EPS) * g


def _split_bf16(x):
    hi = x.astype(BF16)
    lo = (x - hi.astype(F32)).astype(BF16)
    return hi, lo


def _lower_tri(n):
    r = lax.broadcasted_iota(jnp.int32, (n, n), 0)
    c = lax.broadcasted_iota(jnp.int32, (n, n), 1)
    return r >= c


def _chunks(tile, length):
    return [slice(c * length, (c + 1) * length) for c in range(tile // length)]


def _memkv_kernel(mem_ref, g_ref, w_ref, k_ref, v_ref, kb_ref, vb_ref):
    u = _rms(mem_ref[0], g_ref[...]).astype(BF16)
    kv = _dot(u, w_ref[...])
    k = kv[:, :WIDTH]
    v = kv[:, WIDTH:]
    k_ref[0] = k
    v_ref[0] = v
    kb_ref[0] = k.astype(BF16)
    vb_ref[0] = v.astype(BF16)


def _memory_kv(mem, g, w):
    B, M, D = mem.shape
    full = lambda shape: pl.BlockSpec(shape, lambda b: (0,) * len(shape))
    per_b = lambda shape: pl.BlockSpec((1,) + shape, lambda b: (b, 0, 0))
    return pl.pallas_call(
        _memkv_kernel,
        grid=(B,),
        in_specs=[per_b((M, D)), full((1, D)), full((D, 2 * WIDTH))],
        out_specs=[per_b((M, WIDTH))] * 4,
        out_shape=[jax.ShapeDtypeStruct((B, M, WIDTH), F32)] * 2
        + [jax.ShapeDtypeStruct((B, M, WIDTH), BF16)] * 2,
        compiler_params=pltpu.CompilerParams(dimension_semantics=("arbitrary",)),
        name="memory_kv",
    )(mem, g.reshape(1, D), w.astype(BF16))


def _project(x, norm1_ref, wmain_ref, bmain_ref, wg_ref, wgt_ref, bgrow_ref, bgcol_ref,
             z_ref, gcol_ref, grow_ref):
    ub = _rms(x, norm1_ref[...]).astype(BF16)
    n_main = wmain_ref.shape[1]
    gcol_ref[...] = _dot(ub, wg_ref[...]) + bgrow_ref[...]
    grow_ref[...] = _dot_nt(wgt_ref[...], ub) + bgcol_ref[...]
    for c0 in range(0, n_main, COL_GROUP):
        cols = slice(c0, c0 + COL_GROUP)
        z_ref[:, cols] = _dot(ub, wmain_ref[:, cols]) + bmain_ref[:, cols]
        yield


def _hgrn2(z_ref, lbl_ref, hgn_ref, br_ref, st_ref, *, tile, length):
    L = length
    chunks = _chunks(tile, L)
    logits = lbl_ref[...]
    e = jnp.exp(logits - jnp.max(logits, axis=0, keepdims=True))
    lb = e[0:1, :] / jnp.sum(e, axis=0, keepdims=True)
    causal = _lower_tri(L)
    tri = jnp.where(causal, 1.0, 0.0).astype(BF16)

    q, k, a_cum = [], [], []
    for rows in chunks:
        hq = z_ref[rows, _HQ:_HQ + WIDTH]
        f = lb + (1.0 - lb) * _sigmoid(z_ref[rows, _HF:_HF + WIDTH])
        g_hi, g_lo = _split_bf16(jnp.log(f))
        a_cum.append(_dot(tri, g_hi) + _dot(tri, g_lo))
        q.append(hq * _sigmoid(hq))
        k.append(1.0 - f)
    yield

    q_rel, k_rel, q_in, k_out, decay = [], [], [], [], []
    for c in range(len(chunks)):
        a = a_cum[c]
        a_mid = a[L // 2 - 1:L // 2, :]
        a_end = a[L - 1:L, :]
        qr = q[c] * jnp.exp(a - a_mid)
        kr = k[c] * jnp.exp(a_mid - a)
        q_in.append((qr * jnp.exp(a_mid)).astype(BF16))
        k_out.append((kr * jnp.exp(a_end - a_mid)).astype(BF16))
        q_rel.append(qr.astype(BF16))
        k_rel.append(kr.astype(BF16))
        decay.append(jnp.exp(a_end))
    yield

    v = [z_ref[rows, _HI:_HI + WIDTH].astype(BF16) for rows in chunks]
    scores = [[jnp.where(causal, _dot_nt(q_rel[c][:, hs], k_rel[c][:, hs]), 0.0).astype(BF16)
               for hs in _HEAD_COLS] for c in range(len(chunks))]
    update = [[_dot_tn(v[c][:, hs], k_out[c][:, hs]) for hs in _HEAD_COLS] for c in range(len(chunks))]
    yield

    s_in = []
    for h, hs in enumerate(_HEAD_COLS):
        s = st_ref[h]
        per_chunk = []
        for c in range(len(chunks)):
            per_chunk.append(s.astype(BF16))
            s = decay[c][:, hs] * s + update[c][h]
        st_ref[h] = s
        s_in.append(per_chunk)
    yield

    hgn = hgn_ref[...]
    for c, rows in enumerate(chunks):
        gate = z_ref[rows, _HG:_HG + WIDTH]
        gate = gate * _sigmoid(gate)
        for h, hs in enumerate(_HEAD_COLS):
            o = _dot(scores[c][h], v[c][:, hs]) + _dot_nt(q_in[c][:, hs], s_in[h][c])
            br_ref[rows, hs] = (_rms(o, hgn[:, hs]) * gate[:, hs]).astype(BF16)
        yield


def _mlstm(z_ref, gcol_ref, grow_ref, mln_ref, br_ref, c_ref, n_ref, m_ref, *, tile, length):
    L = length
    chunks = _chunks(tile, L)
    nc = len(chunks)
    causal = _lower_tri(L)
    tri = jnp.where(causal, 1.0, 0.0).astype(BF16)
    tri_t = jnp.where(lax.broadcasted_iota(jnp.int32, (L, L), 0) <= lax.broadcasted_iota(jnp.int32, (L, L), 1),
                      1.0, 0.0).astype(BF16)
    lane = lax.broadcasted_iota(jnp.int32, (L, HEAD_DIM), 1)
    sub = lax.broadcasted_iota(jnp.int32, (grow_ref.shape[0], L), 0)

    g_c, g_r, b_c, b_r = [], [], [], []
    for rows in chunks:
        gc = gcol_ref[rows, :]
        gr = grow_ref[:, rows]
        fc_hi, fc_lo = _split_bf16(jnp.where(lane >= HEADS, _log_sigmoid(gc), 0.0))
        fr_hi, fr_lo = _split_bf16(jnp.where(sub >= HEADS, _log_sigmoid(gr), 0.0))
        g_c.append(gc)
        g_r.append(gr)
        b_c.append(_dot(tri, fc_hi) + _dot(tri, fc_lo))
        b_r.append(_dot(fr_hi, tri_t) + _dot(fr_lo, tri_t))
    yield

    blk = [[None] * HEADS for _ in range(nc)]
    for c, rows in enumerate(chunks):
        for h in range(HEADS):
            cols = slice(h * HEAD_DIM, (h + 1) * HEAD_DIM)
            i_col = g_c[c][:, h:h + 1]
            i_row = g_r[c][h:h + 1, :]
            b_col = b_c[c][:, HEADS + h:HEADS + h + 1]
            b_row = b_r[c][HEADS + h:HEADS + h + 1, :]
            b_last = b_row[:, L - 1:L]
            log_w = jnp.where(causal, b_col - b_row + i_row, -jnp.inf)
            q = z_ref[rows, _MQ + cols.start:_MQ + cols.stop]
            k = z_ref[rows, _MK + cols.start:_MK + cols.stop] * (HEAD_DIM ** -0.5)
            qb = q.astype(BF16)
            blk[c][h] = dict(
                b_col=b_col, b_last=b_last, log_w=log_w, q=q, k=k, qb=qb,
                w_max=jnp.max(log_w, axis=-1, keepdims=True),
                last_max=jnp.max(b_last - b_row + i_row, axis=-1, keepdims=True),
                log_last=b_last - b_col + i_col,
                qk=_dot_nt(qb, k.astype(BF16)),
                vb=z_ref[rows, _MV + cols.start:_MV + cols.stop].astype(BF16))
    yield

    for h in range(HEADS):
        m = m_ref[h][:, 0:1]
        for c in range(nc):
            d = blk[c][h]
            d["m_in"] = m
            m = jnp.maximum(d["b_last"] + m, d["last_max"])
            d["m_out"] = m
        m_ref[h] = jnp.broadcast_to(m, (1, HEAD_DIM))
    yield

    for c in range(nc):
        for h in range(HEADS):
            d = blk[c][h]
            log_inter = d["b_col"] + d["m_in"]
            m_t = jnp.maximum(log_inter, d["w_max"])
            d["s"] = d["qk"] * jnp.exp(d["log_w"] - m_t)
            d["a"] = jnp.exp(log_inter - m_t)
            d["floor"] = jnp.exp(-m_t)
            kw = d["k"] * jnp.exp(d["log_last"] - d["m_out"])
            d["dec"] = jnp.exp(d["b_last"] + d["m_in"] - d["m_out"])
            d["c_upd"] = _dot_tn(kw.astype(BF16), d["vb"])
            d["n_upd"] = jnp.sum(kw, axis=0, keepdims=True)
    yield

    for h in range(HEADS):
        cm = c_ref[h]
        n = n_ref[h]
        for c in range(nc):
            d = blk[c][h]
            d["c_in"] = cm.astype(BF16)
            d["n_in"] = n
            cm = d["dec"] * cm + d["c_upd"]
            n = d["dec"] * n + d["n_upd"]
        c_ref[h] = cm
        n_ref[h] = n
    yield

    mln = mln_ref[...]
    for c, rows in enumerate(chunks):
        for h, hs in enumerate(_HEAD_COLS):
            d = blk[c][h]
            num = _dot(d["s"].astype(BF16), d["vb"]) + d["a"] * _dot(d["qb"], d["c_in"])
            den = (jnp.sum(d["s"], axis=-1, keepdims=True)
                   + d["a"] * jnp.sum(d["q"] * d["n_in"], axis=-1, keepdims=True))
            hval = num / jnp.maximum(jnp.abs(den), d["floor"])
            og = z_ref[rows, _MO + hs.start:_MO + hs.stop]
            br_ref[rows, WIDTH + hs.start:WIDTH + hs.stop] = (
                _rms(hval, mln[:, hs]) * _sigmoid(og)).astype(BF16)
        yield


def _xattn(z_ref, mk_ref, mv_ref, br_ref):
    lg = [_dot_nt(z_ref[:, _XQ + hs.start:_XQ + hs.stop].astype(BF16), mk_ref[:, hs]) * (HEAD_DIM ** -0.5)
          for hs in _HEAD_COLS]
    yield
    p = []
    for h in range(HEADS):
        e = jnp.exp(lg[h] - jnp.max(lg[h], axis=-1, keepdims=True))
        p.append((e / jnp.sum(e, axis=-1, keepdims=True)).astype(BF16))
        yield
    for h, hs in enumerate(_HEAD_COLS):
        br_ref[:, 2 * WIDTH + hs.start:2 * WIDTH + hs.stop] = _dot(p[h], mv_ref[:, hs]).astype(BF16)
    yield


def _gates(z_ref, sig_ref):
    for c0 in range(0, sig_ref.shape[1], COL_GROUP):
        sig_ref[:, c0:c0 + COL_GROUP] = _sigmoid(z_ref[:, _GL + c0:_GL + c0 + COL_GROUP])
        yield


def _merge(x_ref, h_ref, br_ref, sig_ref, wbr_ref, wout_ref):
    d_model = x_ref.shape[-1]
    merged = None
    for i in range(N_BRANCH):
        y = sig_ref[:, i * d_model:(i + 1) * d_model] * _dot(br_ref[:, i * WIDTH:(i + 1) * WIDTH], wbr_ref[i])
        merged = y if merged is None else merged + y
        yield
    h_ref[...] = x_ref[...] + _dot(merged.astype(BF16), wout_ref[...])
    yield


_DONE = object()


def _rotate(*streams):
    lead = streams[0][0]
    while True:
        for gen, per_round in streams:
            for _ in range(per_round):
                if next(gen, _DONE) is _DONE and gen is lead:
                    return


def _drain(*gens):
    live = list(gens)
    while live:
        live = [gen for gen in live if next(gen, _DONE) is not _DONE]


def _mixer_kernel(xpair_ref, xnext_ref, s0_ref, c0_ref, n0_ref, m0_ref, mk_ref, mv_ref,
                  norm1_ref, wmain_ref, bmain_ref, wg_ref, wgt_ref, bgrow_ref, bgcol_ref,
                  lbl_ref, hgn_ref, mln_ref, wbr_ref, wout_ref,
                  h_ref, s_out_ref, c_out_ref, n_out_ref, m_out_ref,
                  za_ref, zb_ref, gca_ref, gcb_ref, gra_ref, grb_ref,
                  br_ref, sig_ref, st_ref, c_ref, n_ref, m_ref,
                  *, tile, hg_chunk, ml_chunk, tiles_per_row):
    g = pl.program_id(0)
    proj_w = (norm1_ref, wmain_ref, bmain_ref, wg_ref, wgt_ref, bgrow_ref, bgcol_ref)
    bufs = ((za_ref, gca_ref, gra_ref), (zb_ref, gcb_ref, grb_ref))

    @pl.when(g == 0)
    def _prologue():
        _drain(_project(xpair_ref[0], *proj_w, *bufs[0]))

    for i in range(2):
        tile_in_row = (2 * g + i) % tiles_per_row if tiles_per_row > 1 else 0
        r = i if tiles_per_row == 1 else 0

        def _load_state(r=r):
            for h in range(HEADS):
                st_ref[h] = s0_ref[r, h].T
                c_ref[h] = c0_ref[r, h]
                n_ref[h] = n0_ref[r, h:h + 1, :]
                m_ref[h] = m0_ref[r, h:h + 1, :]

        def _store_state(r=r):
            for h in range(HEADS):
                s_out_ref[r, h] = st_ref[h].T
                c_out_ref[r, h] = c_ref[h]
                n_out_ref[r, h:h + 1, :] = n_ref[h]
                m_out_ref[r, h:h + 1, :] = m_ref[h]

        if tiles_per_row == 1:
            _load_state()
        elif i == 0:
            pl.when(tile_in_row == 0)(_load_state)

        z_ref, gcol_ref, grow_ref = bufs[i]
        x_next = xpair_ref[1] if i == 0 else xnext_ref[0]
        project_next = _project(x_next, *proj_w, *bufs[1 - i])
        hgrn2 = _hgrn2(z_ref, lbl_ref, hgn_ref, br_ref, st_ref, tile=tile, length=hg_chunk)
        mlstm = _mlstm(z_ref, gcol_ref, grow_ref, mln_ref, br_ref, c_ref, n_ref, m_ref,
                       tile=tile, length=ml_chunk)
        xattn = _xattn(z_ref, mk_ref.at[r], mv_ref.at[r], br_ref)
        gates = _gates(z_ref, sig_ref)
        _rotate((hgrn2, 1), (mlstm, 1), (xattn, 1), (gates, 1), (project_next, 1))
        _drain(mlstm, xattn, gates)
        _rotate((_merge(xpair_ref.at[i], h_ref.at[i], br_ref, sig_ref, wbr_ref, wout_ref), 1),
                (project_next, 2))
        _drain(project_next)

        if tiles_per_row == 1:
            _store_state()
        elif i == 1:
            pl.when(tile_in_row == tiles_per_row - 1)(_store_state)


def _mixer(x, s0, c0, n0, m0, mk, mv, p):
    B, T, D = x.shape
    tile = TOKEN_TILE if T % TOKEN_TILE == 0 else T
    hg_chunk = RECURRENCE_CHUNK if tile % RECURRENCE_CHUNK == 0 else tile
    ml_chunk = MLSTM_CHUNK if tile % MLSTM_CHUNK == 0 else tile
    tiles_per_row = T // tile
    n_tiles = B * tiles_per_row
    assert n_tiles % 2 == 0 and (tiles_per_row == 1 or tiles_per_row % 2 == 0)
    rows = 2 if tiles_per_row == 1 else 1
    steps_per_row_block = 1 if tiles_per_row == 1 else tiles_per_row // 2
    n_main = p["w_main"].shape[1]
    n_slots = p["lb_logits"].shape[0]
    M = mk.shape[1]
    gate_rows = p["w_gate_t"].shape[0]

    def full(shape):
        return pl.BlockSpec(shape, lambda g: (0,) * len(shape), pipeline_mode=pl.Buffered(1))

    def per_row(shape):
        return pl.BlockSpec((rows,) + shape, lambda g: (g // steps_per_row_block,) + (0,) * len(shape))

    state = per_row((HEADS, HEAD_DIM, HEAD_DIM))
    vec = per_row((HEADS, HEAD_DIM))
    m0_rep = jnp.broadcast_to(m0[:, :, None], (B, HEADS, HEAD_DIM))
    xt = x.reshape(n_tiles, tile, D)
    kern = functools.partial(_mixer_kernel, tile=tile, hg_chunk=hg_chunk, ml_chunk=ml_chunk,
                             tiles_per_row=tiles_per_row)
    outs = pl.pallas_call(
        kern,
        grid=(n_tiles // 2,),
        in_specs=[
            pl.BlockSpec((2, tile, D), lambda g: (g, 0, 0)),
            pl.BlockSpec((1, tile, D), lambda g: (jnp.minimum(2 * g + 2, n_tiles - 1), 0, 0)),
            state, state, vec, vec, per_row((M, WIDTH)), per_row((M, WIDTH)),
            full((1, D)), full((D, n_main)), full((1, n_main)),
            full((D, HEAD_DIM)), full((gate_rows, D)), full((1, HEAD_DIM)), full((gate_rows, 1)),
            full((n_slots, WIDTH)), full((1, WIDTH)), full((1, WIDTH)),
            full((N_BRANCH, WIDTH, D)), full((D, D)),
        ],
        out_specs=[pl.BlockSpec((2, tile, D), lambda g: (g, 0, 0)), state, state, vec, vec],
        out_shape=[
            jax.ShapeDtypeStruct((n_tiles, tile, D), F32),
            jax.ShapeDtypeStruct((B, HEADS, HEAD_DIM, HEAD_DIM), F32),
            jax.ShapeDtypeStruct((B, HEADS, HEAD_DIM, HEAD_DIM), F32),
            jax.ShapeDtypeStruct((B, HEADS, HEAD_DIM), F32),
            jax.ShapeDtypeStruct((B, HEADS, HEAD_DIM), F32),
        ],
        scratch_shapes=[
            pltpu.VMEM((tile, n_main), F32), pltpu.VMEM((tile, n_main), F32),
            pltpu.VMEM((tile, HEAD_DIM), F32), pltpu.VMEM((tile, HEAD_DIM), F32),
            pltpu.VMEM((gate_rows, tile), F32), pltpu.VMEM((gate_rows, tile), F32),
            pltpu.VMEM((tile, N_BRANCH * WIDTH), BF16),
            pltpu.VMEM((tile, N_BRANCH * D), F32),
            pltpu.VMEM((HEADS, HEAD_DIM, HEAD_DIM), F32),
            pltpu.VMEM((HEADS, HEAD_DIM, HEAD_DIM), F32),
            pltpu.VMEM((HEADS, 1, HEAD_DIM), F32),
            pltpu.VMEM((HEADS, 1, HEAD_DIM), F32),
        ],
        compiler_params=pltpu.CompilerParams(
            dimension_semantics=("arbitrary",), vmem_limit_bytes=VMEM_LIMIT_BYTES),
        name="mixer",
    )(xt, xt, s0, c0, n0, m0_rep, mk, mv,
      p["norm1"], p["w_main"], p["b_main"], p["w_gate"], p["w_gate_t"], p["b_gate_row"], p["b_gate_col"],
      p["lb_logits"], p["hg_norm"], p["ml_norm"], p["w_branch"], p["w_out"])
    return (outs[0].reshape(B, T, D),) + tuple(outs[1:])


def _gelu_tanh(x):
    return 0.5 * x * (1.0 + jnp.tanh(0.7978845608028654 * (x + 0.044715 * (x * x * x))))


def _ffn_kernel(h_ref, cv0_ref, norm2_ref, wup_ref, cw_ref, cb_ref, wdn_ref, fn_ref,
                y_ref, cv_out_ref, ug_ref, act_ref, *, tile, d_ff):
    t = pl.program_id(1)
    last_t = pl.num_programs(1) - 1
    head = 8
    keep = CONV_W - 1

    @pl.when(t == 0)
    def _first():
        ug_ref[head - keep:head, :] = cv0_ref[0]

    @pl.when(t > 0)
    def _carry():
        ug_ref[head - keep:head, :] = ug_ref[head + tile - keep:head + tile, :]

    h = h_ref[0]
    hb = _rms(h, norm2_ref[...]).astype(BF16)
    for c0 in range(0, d_ff, 256):
        cols = slice(c0, c0 + 256)
        ug_ref[head:head + tile, cols] = _dot(hb, wup_ref[:, cols])
    for c0 in range(0, d_ff, 256):
        cols = slice(c0, c0 + 256)
        up_v = _dot(hb, wup_ref[:, d_ff + c0:d_ff + c0 + 256])
        conv = cb_ref[:, cols] + cw_ref[0:1, cols] * ug_ref[head - 2:head - 2 + tile, cols]
        for j in range(1, CONV_W):
            conv = conv + cw_ref[j:j + 1, cols] * ug_ref[head - 2 + j:head - 2 + j + tile, cols]
        act_ref[:, cols] = (_gelu_tanh(conv) * up_v).astype(BF16)
    h2 = h + _dot(act_ref[...], wdn_ref[...])
    y_ref[0] = _rms(h2, fn_ref[...])

    @pl.when(t == last_t)
    def _store():
        cv_out_ref[0] = ug_ref[head + tile - keep:head + tile, :]


def _ffn(h, cv0, p):
    B, T, D = h.shape
    d_ff = p["w_down"].shape[0]
    tile = TOKEN_TILE if T % TOKEN_TILE == 0 else T
    keep = CONV_W - 1

    def full(shape):
        return pl.BlockSpec(shape, lambda b, t: (0,) * len(shape), pipeline_mode=pl.Buffered(1))

    kern = functools.partial(_ffn_kernel, tile=tile, d_ff=d_ff)
    return pl.pallas_call(
        kern,
        grid=(B, T // tile),
        in_specs=[
            pl.BlockSpec((1, tile, D), lambda b, t: (b, t, 0)),
            pl.BlockSpec((1, keep, d_ff), lambda b, t: (b, 0, 0)),
            full((1, D)), full((D, 2 * d_ff)), full((CONV_W, d_ff)), full((1, d_ff)),
            full((d_ff, D)), full((1, D)),
        ],
        out_specs=[pl.BlockSpec((1, tile, D), lambda b, t: (b, t, 0)),
                   pl.BlockSpec((1, keep, d_ff), lambda b, t: (b, 0, 0))],
        out_shape=[jax.ShapeDtypeStruct((B, T, D), F32),
                   jax.ShapeDtypeStruct((B, keep, d_ff), F32)],
        scratch_shapes=[pltpu.VMEM((8 + tile, d_ff), F32),
                        pltpu.VMEM((tile, d_ff), BF16)],
        compiler_params=pltpu.CompilerParams(
            dimension_semantics=("arbitrary", "arbitrary"), vmem_limit_bytes=VMEM_LIMIT_BYTES),
        name="conv_ffn",
    )(h, cv0, p["norm2"], p["w_up"], p["conv_w"], p["conv_b"], p["w_down"], p["final_norm"])


def _prepare(norm1, w_in, b_in, ml_fgate_bias, hg_lb_logits, hg_norm, ml_norm, w_branch, w_out,
             norm2, w_up, ffn_conv_w, ffn_conv_b, w_down, final_norm):
    D = w_in.shape[0]
    g0 = 8 * WIDTH
    g1 = g0 + 2 * HEADS
    w_main = jnp.concatenate([w_in[:, :g0], w_in[:, g1:]], axis=1).astype(BF16)
    b_main = jnp.concatenate([b_in[:g0], b_in[g1:]])[None, :]
    w_gate = jnp.zeros((D, HEAD_DIM), F32).at[:, :2 * HEADS].set(w_in[:, g0:g1]).astype(BF16)
    gate_rows = 16
    w_gate_t = jnp.zeros((gate_rows, D), F32).at[:2 * HEADS].set(w_in[:, g0:g1].T).astype(BF16)
    b_gate = b_in[g0:g1] + jnp.concatenate([jnp.zeros((HEADS,), F32), ml_fgate_bias])
    return dict(
        norm1=norm1[None, :], w_main=w_main, b_main=b_main, w_gate=w_gate, w_gate_t=w_gate_t,
        b_gate_row=jnp.zeros((1, HEAD_DIM), F32).at[0, :2 * HEADS].set(b_gate),
        b_gate_col=jnp.zeros((gate_rows, 1), F32).at[:2 * HEADS, 0].set(b_gate),
        lb_logits=hg_lb_logits, hg_norm=hg_norm[None, :], ml_norm=ml_norm[None, :],
        w_branch=w_branch.astype(BF16), w_out=w_out.astype(BF16),
        norm2=norm2[None, :], w_up=w_up.astype(BF16), conv_w=ffn_conv_w, conv_b=ffn_conv_b[None, :],
        w_down=w_down.astype(BF16), final_norm=final_norm[None, :])


def _layer(x, s_hg, s_c, s_n, s_m, s_cv, mk, mv, p):
    h1, hg, c, n, m = _mixer(x, s_hg, s_c, s_n, s_m, mk, mv, p)
    y, cv = _ffn(h1, s_cv, p)
    return y, hg, c, n, m[:, :, 0], cv


def kernel(x_prompt, x_sample, state_hgrn, state_mlstm_C, state_mlstm_n, state_mlstm_m, state_ffn_conv, cache_mem_k, cache_mem_v, mem_prompt, norm1, w_in, b_in, ml_fgate_bias, hg_lb_logits, hg_norm, ml_norm, mem_norm, w_mem_kv, w_branch, w_out, norm2, w_up, ffn_conv_w, ffn_conv_b, w_down, final_norm):
    depth = norm1.shape[0]
    assert depth == 1, "single-layer encoder"
    assert hg_lb_logits.shape[0] == depth + 1
    Bp = x_prompt.shape[0]
    Bs = x_sample.shape[0]
    M = mem_prompt.shape[1]
    d_ff = w_down.shape[1]
    l = 0
    p = _prepare(norm1[l], w_in[l], b_in[l], ml_fgate_bias[l], hg_lb_logits, hg_norm[l], ml_norm[l],
                 w_branch[l], w_out[l], norm2[l], w_up[l], ffn_conv_w[l], ffn_conv_b[l], w_down[l], final_norm)

    mk_p, mv_p, mk_pb, mv_pb = _memory_kv(mem_prompt, mem_norm[l], w_mem_kv[l])
    zeros = lambda *s: jnp.zeros(s, F32)
    yp, hg_p, c_p, n_p, m_p, cv_p = _layer(
        x_prompt, zeros(Bp, HEADS, HEAD_DIM, HEAD_DIM), zeros(Bp, HEADS, HEAD_DIM, HEAD_DIM),
        zeros(Bp, HEADS, HEAD_DIM), zeros(Bp, HEADS), zeros(Bp, CONV_W - 1, d_ff), mk_pb, mv_pb, p)
    ys, hg_s, c_s, n_s, m_s, cv_s = _layer(
        x_sample, state_hgrn[l], state_mlstm_C[l], state_mlstm_n[l], state_mlstm_m[l], state_ffn_conv[l],
        cache_mem_k[l].reshape(Bs, M, WIDTH).astype(BF16), cache_mem_v[l].reshape(Bs, M, WIDTH).astype(BF16), p)

    mem_shape = (1, Bp, M, HEADS, HEAD_DIM)
    return (yp, ys,
            hg_p[None], c_p[None], n_p[None], m_p[None],
            mk_p.reshape(mem_shape), mv_p.reshape(mem_shape), cv_p[None],
            hg_s[None], c_s[None], n_s[None], m_s[None], cv_s[None])
```

```python
import functools

import jax
import jax.numpy as jnp
from jax import lax
from jax.experimental import pallas as pl
from jax.experimental.pallas import tpu as pltpu

HEADS = 4
HEAD_DIM = 128
WIDTH = HEADS * HEAD_DIM
N_BRANCH = 3
CONV_W = 3
NORM_EPS = 1e-6
RECURRENCE_CHUNK = 64
MLSTM_CHUNK = 128
TOKEN_TILE = 256
FFN_TOKEN_TILE = 512
COL_GROUP = 256
VMEM_LIMIT_BYTES = 60 * 1024 * 1024
HGRN2_LOG_DECAY_RANGE = 60.0

F32 = jnp.float32
BF16 = jnp.bfloat16

_HQ, _HF, _HI, _HG = 0, WIDTH, 2 * WIDTH, 3 * WIDTH
_MQ, _MK, _MV, _MO = 4 * WIDTH, 5 * WIDTH, 6 * WIDTH, 7 * WIDTH
_XQ = 8 * WIDTH
_GL = 9 * WIDTH

_HEAD_COLS = tuple(slice(h * HEAD_DIM, (h + 1) * HEAD_DIM) for h in range(HEADS))


def _dot(a, b):
    return jnp.dot(a, b, preferred_element_type=F32)


def _dot_nt(a, b):
    return lax.dot_general(a, b, (((1,), (1,)), ((), ())), preferred_element_type=F32)


def _dot_tn(a, b):
    return lax.dot_general(a, b, (((0,), (0,)), ((), ())), preferred_element_type=F32)


def _sigmoid(x):
    return 1.0 / (1.0 + jnp.exp(-x))


def _log_sigmoid(x):
    return jnp.minimum(x, 0.0) - jnp.log(1.0 + jnp.exp(-jnp.abs(x)))


def _rms(x, g):
    return x * lax.rsqrt(jnp.mean(x * x, axis=-1, keepdims=True) + NORM_EPS) * g


def _split_bf16(x):
    hi = x.astype(BF16)
    lo = (x - hi.astype(F32)).astype(BF16)
    return hi, lo


def _lower_tri(n):
    r = lax.broadcasted_iota(jnp.int32, (n, n), 0)
    c = lax.broadcasted_iota(jnp.int32, (n, n), 1)
    return r >= c


def _chunks(tile, length):
    return [slice(c * length, (c + 1) * length) for c in range(tile // length)]


def _memkv_kernel(mem_ref, g_ref, w_ref, k_ref, v_ref, kb_ref, vb_ref):
    u = _rms(mem_ref[0], g_ref[...]).astype(BF16)
    kv = _dot(u, w_ref[...])
    k = kv[:, :WIDTH]
    v = kv[:, WIDTH:]
    k_ref[0] = k
    v_ref[0] = v
    kb_ref[0] = k.astype(BF16)
    vb_ref[0] = v.astype(BF16)


def _memory_kv(mem, g, w):
    B, M, D = mem.shape
    full = lambda shape: pl.BlockSpec(shape, lambda b: (0,) * len(shape))
    per_b = lambda shape: pl.BlockSpec((1,) + shape, lambda b: (b, 0, 0))
    return pl.pallas_call(
        _memkv_kernel,
        grid=(B,),
        in_specs=[per_b((M, D)), full((1, D)), full((D, 2 * WIDTH))],
        out_specs=[per_b((M, WIDTH))] * 4,
        out_shape=[jax.ShapeDtypeStruct((B, M, WIDTH), F32)] * 2
        + [jax.ShapeDtypeStruct((B, M, WIDTH), BF16)] * 2,
        compiler_params=pltpu.CompilerParams(dimension_semantics=("arbitrary",)),
        name="memory_kv",
    )(mem, g.reshape(1, D), w.astype(BF16))


def _project(x, norm1_ref, wmain_ref, bmain_ref, wg_ref, wgt_ref, bgrow_ref, bgcol_ref,
             z_ref, gcol_ref, grow_ref):
    ub = _rms(x, norm1_ref[...]).astype(BF16)
    n_main = wmain_ref.shape[1]
    gcol_ref[...] = _dot(ub, wg_ref[...]) + bgrow_ref[...]
    grow_ref[...] = _dot_nt(wgt_ref[...], ub) + bgcol_ref[...]
    for c0 in range(0, n_main, COL_GROUP):
        cols = slice(c0, c0 + COL_GROUP)
        z_ref[:, cols] = _dot(ub, wmain_ref[:, cols]) + bmain_ref[:, cols]
        yield


def _forget_floor(lbl_ref):
    logits = lbl_ref[...]
    e = jnp.exp(logits - jnp.max(logits, axis=0, keepdims=True))
    return e[0:1, :] / jnp.sum(e, axis=0, keepdims=True)


def _hgrn2(z_ref, lbl_ref, hgn_ref, br_ref, st_ref, st_save_ref, min_decay, *, tile, length):
    L = length
    chunks = _chunks(tile, L)
    lb = _forget_floor(lbl_ref)
    causal = _lower_tri(L)
    tri = jnp.where(causal, 1.0, 0.0).astype(BF16)
    for h in range(HEADS):
        st_save_ref[h] = st_ref[h]

    q, k, a_cum = [], [], []
    for rows in chunks:
        hq = z_ref[rows, _HQ:_HQ + WIDTH]
        f = lb + (1.0 - lb) * _sigmoid(z_ref[rows, _HF:_HF + WIDTH])
        g_hi, g_lo = _split_bf16(jnp.log(f))
        a_cum.append(_dot(tri, g_hi) + _dot(tri, g_lo))
        q.append(hq * _sigmoid(hq))
        k.append(1.0 - f)
    yield

    q_rel, k_rel, q_in, k_out, decay = [], [], [], [], []
    for c in range(len(chunks)):
        a = a_cum[c]
        a_mid = a[L // 2 - 1:L // 2, :]
        a_end = a[L - 1:L, :]
        qr = q[c] * jnp.exp(a - a_mid)
        kr = k[c] * jnp.exp(a_mid - a)
        q_in.append((qr * jnp.exp(a_mid)).astype(BF16))
        k_out.append((kr * jnp.exp(a_end - a_mid)).astype(BF16))
        q_rel.append(qr.astype(BF16))
        k_rel.append(kr.astype(BF16))
        decay.append(jnp.exp(a_end))
    ends = a_cum[0][L - 1:L, :]
    for a in a_cum[1:]:
        ends = jnp.minimum(ends, a[L - 1:L, :])
    min_decay.append(jnp.min(ends))
    yield

    v = [z_ref[rows, _HI:_HI + WIDTH].astype(BF16) for rows in chunks]
    scores = [[jnp.where(causal, _dot_nt(q_rel[c][:, hs], k_rel[c][:, hs]), 0.0).astype(BF16)
               for hs in _HEAD_COLS] for c in range(len(chunks))]
    update = [[_dot_tn(v[c][:, hs], k_out[c][:, hs]) for hs in _HEAD_COLS] for c in range(len(chunks))]
    yield

    s_in = []
    for h, hs in enumerate(_HEAD_COLS):
        s = st_ref[h]
        per_chunk = []
        for c in range(len(chunks)):
            per_chunk.append(s.astype(BF16))
            s = decay[c][:, hs] * s + update[c][h]
        st_ref[h] = s
        s_in.append(per_chunk)
    yield

    hgn = hgn_ref[...]
    for c, rows in enumerate(chunks):
        gate = z_ref[rows, _HG:_HG + WIDTH]
        gate = gate * _sigmoid(gate)
        for h, hs in enumerate(_HEAD_COLS):
            o = _dot(scores[c][h], v[c][:, hs]) + _dot_nt(q_in[c][:, hs], s_in[h][c])
            br_ref[rows, hs] = (_rms(o, hgn[:, hs]) * gate[:, hs]).astype(BF16)
        yield


def _hgrn2_per_token(z_ref, lbl_ref, st_ref, st_save_ref, *, tile):
    lb = _forget_floor(lbl_ref)
    for h in range(HEADS):
        st_ref[h] = st_save_ref[h]

    pad = 16
    first_row = lax.broadcasted_iota(jnp.int32, (pad, WIDTH), 0) == 0
    group = 8
    row_id = lax.broadcasted_iota(jnp.int32, (group, WIDTH), 0)

    def token_group(gi, carry):
        rows = pl.ds(pl.multiple_of(gi * group, group), group)
        hq = z_ref[rows, _HQ:_HQ + WIDTH]
        f = lb + (1.0 - lb) * _sigmoid(z_ref[rows, _HF:_HF + WIDTH])
        q = hq * _sigmoid(hq)
        k = 1.0 - f
        v = z_ref[rows, _HI:_HI + WIDTH]
        out = jnp.zeros((group, WIDTH), F32)
        for j in range(group):
            qj = jnp.where(first_row, q[j:j + 1, :], 0.0).astype(BF16)
            kj = jnp.where(first_row, k[j:j + 1, :], 0.0).astype(BF16)
            vj = jnp.where(first_row, v[j:j + 1, :], 0.0).astype(BF16)
            outs = []
            for h, hs in enumerate(_HEAD_COLS):
                s = f[j:j + 1, hs] * st_ref[h] + _dot_tn(vj[:, hs], kj[:, hs])
                st_ref[h] = s
                outs.append(_dot_nt(qj[:, hs], s.astype(BF16))[0:1, :])
            out = jnp.where(row_id == j, jnp.concatenate(outs, axis=-1), out)
        z_ref[rows, _HQ:_HQ + WIDTH] = out
        return carry

    lax.fori_loop(0, tile // group, token_group, 0)


def _redo_tile_per_token(x_ref, h_ref, z_ref, lbl_ref, hgn_ref, br_ref, sig_ref, wbr_ref, wout_ref,
                         st_ref, st_save_ref, *, tile):
    _hgrn2_per_token(z_ref, lbl_ref, st_ref, st_save_ref, tile=tile)
    n_rows = min(tile, RECURRENCE_CHUNK)
    hgn = hgn_ref[...]

    def block(r, carry):
        rows = pl.ds(pl.multiple_of(r * n_rows, n_rows), n_rows)
        gate = z_ref[rows, _HG:_HG + WIDTH]
        gate = gate * _sigmoid(gate)
        for hs in _HEAD_COLS:
            o = z_ref[rows, _HQ + hs.start:_HQ + hs.stop]
            br_ref[rows, hs] = (_rms(o, hgn[:, hs]) * gate[:, hs]).astype(BF16)
        _drain(_merge(x_ref, h_ref, br_ref, sig_ref, wbr_ref, wout_ref, rows=rows))
        return carry

    lax.fori_loop(0, tile // n_rows, block, 0)


def _mlstm(z_ref, gcol_ref, grow_ref, mln_ref, br_ref, c_ref, n_ref, m_ref, *, tile, length):
    L = length
    chunks = _chunks(tile, L)
    nc = len(chunks)
    causal = _lower_tri(L)
    tri = jnp.where(causal, 1.0, 0.0).astype(BF16)
    tri_t = jnp.where(lax.broadcasted_iota(jnp.int32, (L, L), 0) <= lax.broadcasted_iota(jnp.int32, (L, L), 1),
                      1.0, 0.0).astype(BF16)
    lane = lax.broadcasted_iota(jnp.int32, (L, HEAD_DIM), 1)
    sub = lax.broadcasted_iota(jnp.int32, (grow_ref.shape[0], L), 0)

    g_c, g_r, b_c, b_r = [], [], [], []
    for rows in chunks:
        gc = gcol_ref[rows, :]
        gr = grow_ref[:, rows]
        fc_hi, fc_lo = _split_bf16(jnp.where(lane >= HEADS, _log_sigmoid(gc), 0.0))
        fr_hi, fr_lo = _split_bf16(jnp.where(sub >= HEADS, _log_sigmoid(gr), 0.0))
        g_c.append(gc)
        g_r.append(gr)
        b_c.append(_dot(tri, fc_hi) + _dot(tri, fc_lo))
        b_r.append(_dot(fr_hi, tri_t) + _dot(fr_lo, tri_t))
    yield

    blk = [[None] * HEADS for _ in range(nc)]
    for c, rows in enumerate(chunks):
        for h in range(HEADS):
            cols = slice(h * HEAD_DIM, (h + 1) * HEAD_DIM)
            i_col = g_c[c][:, h:h + 1]
            i_row = g_r[c][h:h + 1, :]
            b_col = b_c[c][:, HEADS + h:HEADS + h + 1]
            b_row = b_r[c][HEADS + h:HEADS + h + 1, :]
            b_last = b_row[:, L - 1:L]
            log_w = jnp.where(causal, b_col - b_row + i_row, -jnp.inf)
            q = z_ref[rows, _MQ + cols.start:_MQ + cols.stop]
            k = z_ref[rows, _MK + cols.start:_MK + cols.stop] * (HEAD_DIM ** -0.5)
            qb = q.astype(BF16)
            blk[c][h] = dict(
                b_col=b_col, b_last=b_last, log_w=log_w, q=q, k=k, qb=qb,
                w_max=jnp.max(log_w, axis=-1, keepdims=True),
                last_max=jnp.max(b_last - b_row + i_row, axis=-1, keepdims=True),
                log_last=b_last - b_col + i_col,
                qk=_dot_nt(qb, k.astype(BF16)),
                vb=z_ref[rows, _MV + cols.start:_MV + cols.stop].astype(BF16))
    yield

    for h in range(HEADS):
        m = m_ref[h][:, 0:1]
        for c in range(nc):
            d = blk[c][h]
            d["m_in"] = m
            m = jnp.maximum(d["b_last"] + m, d["last_max"])
            d["m_out"] = m
        m_ref[h] = jnp.broadcast_to(m, (1, HEAD_DIM))
    yield

    for c in range(nc):
        for h in range(HEADS):
            d = blk[c][h]
            log_inter = d["b_col"] + d["m_in"]
            m_t = jnp.maximum(log_inter, d["w_max"])
            d["s"] = d["qk"] * jnp.exp(d["log_w"] - m_t)
            d["a"] = jnp.exp(log_inter - m_t)
            d["floor"] = jnp.exp(-m_t)
            kw = d["k"] * jnp.exp(d["log_last"] - d["m_out"])
            d["dec"] = jnp.exp(d["b_last"] + d["m_in"] - d["m_out"])
            d["c_upd"] = _dot_tn(kw.astype(BF16), d["vb"])
            d["n_upd"] = jnp.sum(kw, axis=0, keepdims=True)
    yield

    for h in range(HEADS):
        cm = c_ref[h]
        n = n_ref[h]
        for c in range(nc):
            d = blk[c][h]
            d["c_in"] = cm.astype(BF16)
            d["n_in"] = n
            cm = d["dec"] * cm + d["c_upd"]
            n = d["dec"] * n + d["n_upd"]
        c_ref[h] = cm
        n_ref[h] = n
    yield

    mln = mln_ref[...]
    for c, rows in enumerate(chunks):
        for h, hs in enumerate(_HEAD_COLS):
            d = blk[c][h]
            num = _dot(d["s"].astype(BF16), d["vb"]) + d["a"] * _dot(d["qb"], d["c_in"])
            den = (jnp.sum(d["s"], axis=-1, keepdims=True)
                   + d["a"] * jnp.sum(d["q"] * d["n_in"], axis=-1, keepdims=True))
            hval = num / jnp.maximum(jnp.abs(den), d["floor"])
            og = z_ref[rows, _MO + hs.start:_MO + hs.stop]
            br_ref[rows, WIDTH + hs.start:WIDTH + hs.stop] = (
                _rms(hval, mln[:, hs]) * _sigmoid(og)).astype(BF16)
        yield


def _xattn(z_ref, mk_ref, mv_ref, br_ref):
    lg = [_dot_nt(z_ref[:, _XQ + hs.start:_XQ + hs.stop].astype(BF16), mk_ref[:, hs]) * (HEAD_DIM ** -0.5)
          for hs in _HEAD_COLS]
    yield
    p = []
    for h in range(HEADS):
        e = jnp.exp(lg[h] - jnp.max(lg[h], axis=-1, keepdims=True))
        p.append((e / jnp.sum(e, axis=-1, keepdims=True)).astype(BF16))
        yield
    for h, hs in enumerate(_HEAD_COLS):
        br_ref[:, 2 * WIDTH + hs.start:2 * WIDTH + hs.stop] = _dot(p[h], mv_ref[:, hs]).astype(BF16)
    yield


def _gates(z_ref, sig_ref):
    for c0 in range(0, sig_ref.shape[1], COL_GROUP):
        sig_ref[:, c0:c0 + COL_GROUP] = _sigmoid(z_ref[:, _GL + c0:_GL + c0 + COL_GROUP])
        yield


def _merge(x_ref, h_ref, br_ref, sig_ref, wbr_ref, wout_ref, rows=slice(None)):
    d_model = x_ref.shape[-1]
    merged = None
    for i in range(N_BRANCH):
        y = (sig_ref[rows, i * d_model:(i + 1) * d_model]
             * _dot(br_ref[rows, i * WIDTH:(i + 1) * WIDTH], wbr_ref[i]))
        merged = y if merged is None else merged + y
        yield
    h_ref[rows, :] = x_ref[rows, :] + _dot(merged.astype(BF16), wout_ref[...])
    yield


_DONE = object()


def _rotate(*streams):
    lead = streams[0][0]
    while True:
        for gen, per_round in streams:
            for _ in range(per_round):
                if next(gen, _DONE) is _DONE and gen is lead:
                    return


def _drain(*gens):
    live = list(gens)
    while live:
        live = [gen for gen in live if next(gen, _DONE) is not _DONE]


def _mixer_kernel(xpair_ref, xnext_ref, s0_ref, c0_ref, n0_ref, m0_ref, mk_ref, mv_ref,
                  norm1_ref, wmain_ref, bmain_ref, wg_ref, wgt_ref, bgrow_ref, bgcol_ref,
                  lbl_ref, hgn_ref, mln_ref, wbr_ref, wout_ref,
                  h_ref, s_out_ref, c_out_ref, n_out_ref, m_out_ref,
                  za_ref, zb_ref, gca_ref, gcb_ref, gra_ref, grb_ref,
                  br_ref, st_ref, st_save_ref, c_ref, n_ref, m_ref,
                  *, tile, hg_chunk, ml_chunk, tiles_per_row):
    g = pl.program_id(0)
    proj_w = (norm1_ref, wmain_ref, bmain_ref, wg_ref, wgt_ref, bgrow_ref, bgcol_ref)
    bufs = ((za_ref, gca_ref, gra_ref), (zb_ref, gcb_ref, grb_ref))

    @pl.when(g == 0)
    def _prologue():
        _drain(_project(xpair_ref[0], *proj_w, *bufs[0]))

    for i in range(2):
        tile_in_row = (2 * g + i) % tiles_per_row if tiles_per_row > 1 else 0
        r = i if tiles_per_row == 1 else 0

        def _load_state(r=r):
            for h in range(HEADS):
                st_ref[h] = s0_ref[r, h].T
                c_ref[h] = c0_ref[r, h]
                n_ref[h] = n0_ref[r, h:h + 1, :]
                m_ref[h] = m0_ref[r, h:h + 1, :]

        def _store_state(r=r):
            for h in range(HEADS):
                s_out_ref[r, h] = st_ref[h].T
                c_out_ref[r, h] = c_ref[h]
                n_out_ref[r, h:h + 1, :] = n_ref[h]
                m_out_ref[r, h:h + 1, :] = m_ref[h]

        if tiles_per_row == 1:
            _load_state()
        elif i == 0:
            pl.when(tile_in_row == 0)(_load_state)

        z_ref, gcol_ref, grow_ref = bufs[i]
        x_next = xpair_ref[1] if i == 0 else xnext_ref[0]
        project_next = _project(x_next, *proj_w, *bufs[1 - i])
        min_decay = []
        hgrn2 = _hgrn2(z_ref, lbl_ref, hgn_ref, br_ref, st_ref, st_save_ref, min_decay,
                       tile=tile, length=hg_chunk)
        mlstm = _mlstm(z_ref, gcol_ref, grow_ref, mln_ref, br_ref, c_ref, n_ref, m_ref,
                       tile=tile, length=ml_chunk)
        xattn = _xattn(z_ref, mk_ref.at[r], mv_ref.at[r], br_ref)
        sig_ref = z_ref.at[:, _GL:wmain_ref.shape[1]]
        gates = _gates(z_ref, sig_ref)
        _rotate((hgrn2, 1), (mlstm, 1), (xattn, 1), (gates, 1), (project_next, 2))
        _drain(mlstm, xattn, gates)
        _rotate((_merge(xpair_ref.at[i], h_ref.at[i], br_ref, sig_ref, wbr_ref, wout_ref), 1),
                (project_next, 4))
        _drain(project_next)

        @pl.when(min_decay[0] < -HGRN2_LOG_DECAY_RANGE)
        def _redo_hgrn2(i=i, z_ref=z_ref):
            _redo_tile_per_token(xpair_ref.at[i], h_ref.at[i], z_ref, lbl_ref, hgn_ref, br_ref, sig_ref,
                                 wbr_ref, wout_ref, st_ref, st_save_ref, tile=tile)

        if tiles_per_row == 1:
            _store_state()
        elif i == 1:
            pl.when(tile_in_row == tiles_per_row - 1)(_store_state)


def _mixer(x, s0, c0, n0, m0, mk, mv, p):
    B, T, D = x.shape
    tile = TOKEN_TILE if T % TOKEN_TILE == 0 else T
    hg_chunk = RECURRENCE_CHUNK if tile % RECURRENCE_CHUNK == 0 else tile
    ml_chunk = MLSTM_CHUNK if tile % MLSTM_CHUNK == 0 else tile
    tiles_per_row = T // tile
    n_tiles = B * tiles_per_row
    assert n_tiles % 2 == 0 and (tiles_per_row == 1 or tiles_per_row % 2 == 0)
    rows = 2 if tiles_per_row == 1 else 1
    steps_per_row_block = 1 if tiles_per_row == 1 else tiles_per_row // 2
    n_main = p["w_main"].shape[1]
    n_slots = p["lb_logits"].shape[0]
    M = mk.shape[1]
    gate_rows = p["w_gate_t"].shape[0]

    def full(shape):
        return pl.BlockSpec(shape, lambda g: (0,) * len(shape), pipeline_mode=pl.Buffered(1))

    def per_row(shape):
        return pl.BlockSpec((rows,) + shape, lambda g: (g // steps_per_row_block,) + (0,) * len(shape))

    state = per_row((HEADS, HEAD_DIM, HEAD_DIM))
    vec = per_row((HEADS, HEAD_DIM))
    m0_rep = jnp.broadcast_to(m0[:, :, None], (B, HEADS, HEAD_DIM))
    xt = x.reshape(n_tiles, tile, D)
    kern = functools.partial(_mixer_kernel, tile=tile, hg_chunk=hg_chunk, ml_chunk=ml_chunk,
                             tiles_per_row=tiles_per_row)
    outs = pl.pallas_call(
        kern,
        grid=(n_tiles // 2,),
        in_specs=[
            pl.BlockSpec((2, tile, D), lambda g: (g, 0, 0)),
            pl.BlockSpec((1, tile, D), lambda g: (jnp.minimum(2 * g + 2, n_tiles - 1), 0, 0)),
            state, state, vec, vec, per_row((M, WIDTH)), per_row((M, WIDTH)),
            full((1, D)), full((D, n_main)), full((1, n_main)),
            full((D, HEAD_DIM)), full((gate_rows, D)), full((1, HEAD_DIM)), full((gate_rows, 1)),
            full((n_slots, WIDTH)), full((1, WIDTH)), full((1, WIDTH)),
            full((N_BRANCH, WIDTH, D)), full((D, D)),
        ],
        out_specs=[pl.BlockSpec((2, tile, D), lambda g: (g, 0, 0)), state, state, vec, vec],
        out_shape=[
            jax.ShapeDtypeStruct((n_tiles, tile, D), F32),
            jax.ShapeDtypeStruct((B, HEADS, HEAD_DIM, HEAD_DIM), F32),
            jax.ShapeDtypeStruct((B, HEADS, HEAD_DIM, HEAD_DIM), F32),
            jax.ShapeDtypeStruct((B, HEADS, HEAD_DIM), F32),
            jax.ShapeDtypeStruct((B, HEADS, HEAD_DIM), F32),
        ],
        scratch_shapes=[
            pltpu.VMEM((tile, n_main), F32), pltpu.VMEM((tile, n_main), F32),
            pltpu.VMEM((tile, HEAD_DIM), F32), pltpu.VMEM((tile, HEAD_DIM), F32),
            pltpu.VMEM((gate_rows, tile), F32), pltpu.VMEM((gate_rows, tile), F32),
            pltpu.VMEM((tile, N_BRANCH * WIDTH), BF16),
            pltpu.VMEM((HEADS, HEAD_DIM, HEAD_DIM), F32),
            pltpu.VMEM((HEADS, HEAD_DIM, HEAD_DIM), F32),
            pltpu.VMEM((HEADS, HEAD_DIM, HEAD_DIM), F32),
            pltpu.VMEM((HEADS, 1, HEAD_DIM), F32),
            pltpu.VMEM((HEADS, 1, HEAD_DIM), F32),
        ],
        compiler_params=pltpu.CompilerParams(
            dimension_semantics=("arbitrary",), vmem_limit_bytes=VMEM_LIMIT_BYTES),
        name="mixer",
    )(xt, xt, s0, c0, n0, m0_rep, mk, mv,
      p["norm1"], p["w_main"], p["b_main"], p["w_gate"], p["w_gate_t"], p["b_gate_row"], p["b_gate_col"],
      p["lb_logits"], p["hg_norm"], p["ml_norm"], p["w_branch"], p["w_out"])
    return (outs[0].reshape(B, T, D),) + tuple(outs[1:])


_GELU_C = 0.7978845608028654


def _gelu_tanh_times(x, y):
    inner = x * (_GELU_C + (_GELU_C * 0.044715) * (x * x))
    return (0.5 + 0.5 * jnp.tanh(inner)) * (x * y)


def _ffn_kernel(h_ref, cv0_ref, norm2_ref, wup_ref, cw_ref, cb_ref, wdn_ref, fn_ref,
                y_ref, cv_out_ref, ug_ref, act_ref, *, tile, d_ff):
    t = pl.program_id(1)
    last_t = pl.num_programs(1) - 1
    head = 8
    keep = CONV_W - 1

    @pl.when(t == 0)
    def _first():
        ug_ref[0:head - keep, :] = jnp.zeros((head - keep, d_ff), F32)
        ug_ref[head - keep:head, :] = cv0_ref[0]

    @pl.when(t > 0)
    def _carry():
        ug_ref[head - keep:head, :] = ug_ref[head + tile - keep:head + tile, :]

    h = h_ref[0]
    hb = _rms(h, norm2_ref[...]).astype(BF16)
    for c0 in range(0, d_ff, 256):
        cols = slice(c0, c0 + 256)
        ug_ref[head:head + tile, cols] = _dot(hb, wup_ref[:, cols])
        up_v = _dot(hb, wup_ref[:, d_ff + c0:d_ff + c0 + 256])
        rows = ug_ref[:, cols]
        conv = cb_ref[:, cols] + cw_ref[CONV_W - 1:CONV_W, cols] * rows[head:, :]
        for j in range(1, CONV_W):
            conv = conv + cw_ref[CONV_W - 1 - j:CONV_W - j, cols] * pltpu.roll(rows, j, axis=0)[head:, :]
        act_ref[:, cols] = _gelu_tanh_times(conv, up_v).astype(BF16)
    h2 = h + _dot(act_ref[...], wdn_ref[...])
    y_ref[0] = _rms(h2, fn_ref[...])

    @pl.when(t == last_t)
    def _store():
        cv_out_ref[0] = ug_ref[head + tile - keep:head + tile, :]


def _ffn(h, cv0, p):
    B, T, D = h.shape
    d_ff = p["w_down"].shape[0]
    tile = FFN_TOKEN_TILE if T % FFN_TOKEN_TILE == 0 else T
    keep = CONV_W - 1

    def full(shape):
        return pl.BlockSpec(shape, lambda b, t: (0,) * len(shape), pipeline_mode=pl.Buffered(1))

    kern = functools.partial(_ffn_kernel, tile=tile, d_ff=d_ff)
    return pl.pallas_call(
        kern,
        grid=(B, T // tile),
        in_specs=[
            pl.BlockSpec((1, tile, D), lambda b, t: (b, t, 0)),
            pl.BlockSpec((1, keep, d_ff), lambda b, t: (b, 0, 0)),
            full((1, D)), full((D, 2 * d_ff)), full((CONV_W, d_ff)), full((1, d_ff)),
            full((d_ff, D)), full((1, D)),
        ],
        out_specs=[pl.BlockSpec((1, tile, D), lambda b, t: (b, t, 0)),
                   pl.BlockSpec((1, keep, d_ff), lambda b, t: (b, 0, 0))],
        out_shape=[jax.ShapeDtypeStruct((B, T, D), F32),
                   jax.ShapeDtypeStruct((B, keep, d_ff), F32)],
        scratch_shapes=[pltpu.VMEM((8 + tile, d_ff), F32),
                        pltpu.VMEM((tile, d_ff), BF16)],
        compiler_params=pltpu.CompilerParams(
            dimension_semantics=("arbitrary", "arbitrary"), vmem_limit_bytes=VMEM_LIMIT_BYTES),
        name="conv_ffn",
    )(h, cv0, p["norm2"], p["w_up"], p["conv_w"], p["conv_b"], p["w_down"], p["final_norm"])


def _prepare(norm1, w_in, b_in, ml_fgate_bias, hg_lb_logits, hg_norm, ml_norm, w_branch, w_out,
             norm2, w_up, ffn_conv_w, ffn_conv_b, w_down, final_norm):
    D = w_in.shape[0]
    g0 = 8 * WIDTH
    g1 = g0 + 2 * HEADS
    w_main = jnp.concatenate([w_in[:, :g0], w_in[:, g1:]], axis=1).astype(BF16)
    b_main = jnp.concatenate([b_in[:g0], b_in[g1:]])[None, :]
    w_gate = jnp.zeros((D, HEAD_DIM), F32).at[:, :2 * HEADS].set(w_in[:, g0:g1]).astype(BF16)
    gate_rows = 16
    w_gate_t = jnp.zeros((gate_rows, D), F32).at[:2 * HEADS].set(w_in[:, g0:g1].T).astype(BF16)
    b_gate = b_in[g0:g1] + jnp.concatenate([jnp.zeros((HEADS,), F32), ml_fgate_bias])
    return dict(
        norm1=norm1[None, :], w_main=w_main, b_main=b_main, w_gate=w_gate, w_gate_t=w_gate_t,
        b_gate_row=jnp.zeros((1, HEAD_DIM), F32).at[0, :2 * HEADS].set(b_gate),
        b_gate_col=jnp.zeros((gate_rows, 1), F32).at[:2 * HEADS, 0].set(b_gate),
        lb_logits=hg_lb_logits, hg_norm=hg_norm[None, :], ml_norm=ml_norm[None, :],
        w_branch=w_branch.astype(BF16), w_out=w_out.astype(BF16),
        norm2=norm2[None, :], w_up=w_up.astype(BF16), conv_w=ffn_conv_w, conv_b=ffn_conv_b[None, :],
        w_down=w_down.astype(BF16), final_norm=final_norm[None, :])


def _layer(x, s_hg, s_c, s_n, s_m, s_cv, mk, mv, p):
    h1, hg, c, n, m = _mixer(x, s_hg, s_c, s_n, s_m, mk, mv, p)
    y, cv = _ffn(h1, s_cv, p)
    return y, hg, c, n, m[:, :, 0], cv


def kernel(x_prompt, x_sample, state_hgrn, state_mlstm_C, state_mlstm_n, state_mlstm_m, state_ffn_conv, cache_mem_k, cache_mem_v, mem_prompt, norm1, w_in, b_in, ml_fgate_bias, hg_lb_logits, hg_norm, ml_norm, mem_norm, w_mem_kv, w_branch, w_out, norm2, w_up, ffn_conv_w, ffn_conv_b, w_down, final_norm):
    depth = norm1.shape[0]
    assert depth == 1, "single-layer encoder"
    assert hg_lb_logits.shape[0] == depth + 1
    Bp = x_prompt.shape[0]
    Bs = x_sample.shape[0]
    M = mem_prompt.shape[1]
    d_ff = w_down.shape[1]
    l = 0
    p = _prepare(norm1[l], w_in[l], b_in[l], ml_fgate_bias[l], hg_lb_logits, hg_norm[l], ml_norm[l],
                 w_branch[l], w_out[l], norm2[l], w_up[l], ffn_conv_w[l], ffn_conv_b[l], w_down[l], final_norm)

    mk_p, mv_p, mk_pb, mv_pb = _memory_kv(mem_prompt, mem_norm[l], w_mem_kv[l])
    zeros = lambda *s: jnp.zeros(s, F32)
    yp, hg_p, c_p, n_p, m_p, cv_p = _layer(
        x_prompt, zeros(Bp, HEADS, HEAD_DIM, HEAD_DIM), zeros(Bp, HEADS, HEAD_DIM, HEAD_DIM),
        zeros(Bp, HEADS, HEAD_DIM), zeros(Bp, HEADS), zeros(Bp, CONV_W - 1, d_ff), mk_pb, mv_pb, p)
    ys, hg_s, c_s, n_s, m_s, cv_s = _layer(
        x_sample, state_hgrn[l], state_mlstm_C[l], state_mlstm_n[l], state_mlstm_m[l], state_ffn_conv[l],
        cache_mem_k[l].reshape(Bs, M, WIDTH).astype(BF16), cache_mem_v[l].reshape(Bs, M, WIDTH).astype(BF16), p)

    mem_shape = (1, Bp, M, HEADS, HEAD_DIM)
    return (yp, ys,
            hg_p[None], c_p[None], n_p[None], m_p[None],
            mk_p.reshape(mem_shape), mv_p.reshape(mem_shape), cv_p[None],
            hg_s[None], c_s[None], n_s[None], m_s[None], cv_s[None])
```

```python
import functools

import jax
import jax.numpy as jnp
from jax import lax
from jax.experimental import pallas as pl
from jax.experimental.pallas import tpu as pltpu

HEADS = 4
HEAD_DIM = 128
WIDTH = HEADS * HEAD_DIM
N_BRANCH = 3
CONV_W = 3
NORM_EPS = 1e-6
RECURRENCE_CHUNK = 64
MLSTM_CHUNK = 128
TOKEN_TILE = 256
FFN_TOKEN_TILE = 1024
COL_GROUP = 256
NORM_ROWS = 64
VMEM_LIMIT_BYTES = 60 * 1024 * 1024
HGRN2_LOG_DECAY_RANGE = 60.0

F32 = jnp.float32
BF16 = jnp.bfloat16

_HQ, _HF, _HI, _HG = 0, WIDTH, 2 * WIDTH, 3 * WIDTH
_MQ, _MK, _MV, _MO = 4 * WIDTH, 5 * WIDTH, 6 * WIDTH, 7 * WIDTH
_XQ = 8 * WIDTH
_GL = 9 * WIDTH

_HEAD_COLS = tuple(slice(h * HEAD_DIM, (h + 1) * HEAD_DIM) for h in range(HEADS))


def _dot(a, b):
    return jnp.dot(a, b, preferred_element_type=F32)


def _dot_nt(a, b):
    return lax.dot_general(a, b, (((1,), (1,)), ((), ())), preferred_element_type=F32)


def _dot_tn(a, b):
    return lax.dot_general(a, b, (((0,), (0,)), ((), ())), preferred_element_type=F32)


def _sigmoid(x):
    return 1.0 / (1.0 + jnp.exp(-x))


def _log_sigmoid(x):
    return jnp.minimum(x, 0.0) - jnp.log(1.0 + jnp.exp(-jnp.abs(x)))


def _rms(x, g):
    return x * lax.rsqrt(jnp.mean(x * x, axis=-1, keepdims=True) + NORM_EPS) * g


def _split_bf16(x):
    hi = x.astype(BF16)
    lo = (x - hi.astype(F32)).astype(BF16)
    return hi, lo


def _lower_tri(n):
    r = lax.broadcasted_iota(jnp.int32, (n, n), 0)
    c = lax.broadcasted_iota(jnp.int32, (n, n), 1)
    return r >= c


def _chunks(tile, length):
    return [slice(c * length, (c + 1) * length) for c in range(tile // length)]


def _memkv_kernel(mem_ref, g_ref, w_ref, k_ref, v_ref, kb_ref, vb_ref):
    u = _rms(mem_ref[0], g_ref[...]).astype(BF16)
    kv = _dot(u, w_ref[...])
    k = kv[:, :WIDTH]
    v = kv[:, WIDTH:]
    for h, hs in enumerate(_HEAD_COLS):
        k_ref[0, :, h, :] = k[:, hs]
        v_ref[0, :, h, :] = v[:, hs]
    kb_ref[0] = k.astype(BF16)
    vb_ref[0] = v.astype(BF16)


def _memory_kv(mem, g, w):
    B, M, D = mem.shape
    full = lambda shape: pl.BlockSpec(shape, lambda b: (0,) * len(shape))
    per_b = lambda shape: pl.BlockSpec((1,) + shape, lambda b: (b, 0, 0))
    return pl.pallas_call(
        _memkv_kernel,
        grid=(B,),
        in_specs=[per_b((M, D)), full((1, D)), full((D, 2 * WIDTH))],
        out_specs=[pl.BlockSpec((1, M, HEADS, HEAD_DIM), lambda b: (b, 0, 0, 0))] * 2 + [per_b((M, WIDTH))] * 2,
        out_shape=[jax.ShapeDtypeStruct((B, M, HEADS, HEAD_DIM), F32)] * 2
        + [jax.ShapeDtypeStruct((B, M, WIDTH), BF16)] * 2,
        compiler_params=pltpu.CompilerParams(dimension_semantics=("arbitrary",)),
        name="memory_kv",
    )(mem, g.reshape(1, D), w.astype(BF16))


def _normalise(x_ref, norm1_ref, ub_ref):
    n_rows = min(x_ref.shape[0], NORM_ROWS)
    for r0 in range(0, x_ref.shape[0], n_rows):
        rows = slice(r0, r0 + n_rows)
        ub_ref[rows, :] = _rms(x_ref[rows, :], norm1_ref[...]).astype(BF16)
        yield


def _project(ub_ref, wmain_ref, bmain_ref, wg_ref, wgt_ref, bgrow_ref, bgcol_ref,
             z_ref, sig_ref, gcol_ref, grow_ref):
    ub = ub_ref[...]
    n_main = wmain_ref.shape[1]
    gcol_ref[...] = _dot(ub, wg_ref[...]) + bgrow_ref[...]
    grow_ref[...] = _dot_nt(wgt_ref[...], ub) + bgcol_ref[...]
    for c0 in range(0, n_main, COL_GROUP):
        cols = slice(c0, c0 + COL_GROUP)
        zc = _dot(ub, wmain_ref[:, cols]) + bmain_ref[:, cols]
        if c0 < _GL:
            z_ref[:, cols] = zc
        else:
            sig_ref[:, c0 - _GL:c0 - _GL + COL_GROUP] = _sigmoid(zc).astype(BF16)
        yield


def _forget_floor(lbl_ref):
    logits = lbl_ref[...]
    e = jnp.exp(logits - jnp.max(logits, axis=0, keepdims=True))
    return e[0:1, :] / jnp.sum(e, axis=0, keepdims=True)


def _hgrn2(z_ref, lbl_ref, hgn_ref, br_ref, st_ref, st_save_ref, min_decay, *, tile, length):
    L = length
    chunks = _chunks(tile, L)
    lb = _forget_floor(lbl_ref)
    causal = _lower_tri(L)
    tri = jnp.where(causal, 1.0, 0.0).astype(BF16)
    for h in range(HEADS):
        st_save_ref[h] = st_ref[h]

    k, a_cum = [], []
    for rows in chunks:
        f = lb + (1.0 - lb) * _sigmoid(z_ref[rows, _HF:_HF + WIDTH])
        g_hi, g_lo = _split_bf16(jnp.log(f))
        a_cum.append(_dot(tri, g_hi) + _dot(tri, g_lo))
        k.append(1.0 - f)
        yield

    q_rel, k_rel, q_in, k_out, decay = [], [], [], [], []
    for c, rows in enumerate(chunks):
        a = a_cum[c]
        a_mid = a[L // 2 - 1:L // 2, :]
        a_end = a[L - 1:L, :]
        hq = z_ref[rows, _HQ:_HQ + WIDTH]
        qr = hq * _sigmoid(hq) * jnp.exp(a - a_mid)
        kr = k[c] * jnp.exp(a_mid - a)
        q_in.append((qr * jnp.exp(a_mid)).astype(BF16))
        k_out.append((kr * jnp.exp(a_end - a_mid)).astype(BF16))
        q_rel.append(qr.astype(BF16))
        k_rel.append(kr.astype(BF16))
        decay.append(jnp.exp(a_end))
        yield
    ends = a_cum[0][L - 1:L, :]
    for a in a_cum[1:]:
        ends = jnp.minimum(ends, a[L - 1:L, :])
    min_decay.append(jnp.min(ends))
    yield

    v = [z_ref[rows, _HI:_HI + WIDTH].astype(BF16) for rows in chunks]
    scores = [[jnp.where(causal, _dot_nt(q_rel[c][:, hs], k_rel[c][:, hs]), 0.0).astype(BF16)
               for hs in _HEAD_COLS] for c in range(len(chunks))]
    update = [[_dot_tn(v[c][:, hs], k_out[c][:, hs]) for hs in _HEAD_COLS] for c in range(len(chunks))]
    yield

    s_in = []
    for h, hs in enumerate(_HEAD_COLS):
        s = st_ref[h]
        per_chunk = []
        for c in range(len(chunks)):
            per_chunk.append(s.astype(BF16))
            s = decay[c][:, hs] * s + update[c][h]
        st_ref[h] = s
        s_in.append(per_chunk)
        yield

    hgn = hgn_ref[...]
    for c, rows in enumerate(chunks):
        gate = z_ref[rows, _HG:_HG + WIDTH]
        gate = gate * _sigmoid(gate)
        for h, hs in enumerate(_HEAD_COLS):
            o = _dot(scores[c][h], v[c][:, hs]) + _dot_nt(q_in[c][:, hs], s_in[h][c])
            br_ref[rows, hs] = (_rms(o, hgn[:, hs]) * gate[:, hs]).astype(BF16)
        yield


def _hgrn2_per_token(z_ref, lbl_ref, st_ref, st_save_ref, *, tile):
    lb = _forget_floor(lbl_ref)
    for h in range(HEADS):
        st_ref[h] = st_save_ref[h]

    pad = 16
    first_row = lax.broadcasted_iota(jnp.int32, (pad, WIDTH), 0) == 0
    group = 8
    row_id = lax.broadcasted_iota(jnp.int32, (group, WIDTH), 0)

    def token_group(gi, carry):
        rows = pl.ds(pl.multiple_of(gi * group, group), group)
        hq = z_ref[rows, _HQ:_HQ + WIDTH]
        f = lb + (1.0 - lb) * _sigmoid(z_ref[rows, _HF:_HF + WIDTH])
        q = hq * _sigmoid(hq)
        k = 1.0 - f
        v = z_ref[rows, _HI:_HI + WIDTH]
        out = jnp.zeros((group, WIDTH), F32)
        for j in range(group):
            qj = jnp.where(first_row, q[j:j + 1, :], 0.0).astype(BF16)
            kj = jnp.where(first_row, k[j:j + 1, :], 0.0).astype(BF16)
            vj = jnp.where(first_row, v[j:j + 1, :], 0.0).astype(BF16)
            outs = []
            for h, hs in enumerate(_HEAD_COLS):
                s = f[j:j + 1, hs] * st_ref[h] + _dot_tn(vj[:, hs], kj[:, hs])
                st_ref[h] = s
                outs.append(_dot_nt(qj[:, hs], s.astype(BF16))[0:1, :])
            out = jnp.where(row_id == j, jnp.concatenate(outs, axis=-1), out)
        z_ref[rows, _HQ:_HQ + WIDTH] = out
        return carry

    lax.fori_loop(0, tile // group, token_group, 0)


def _redo_tile_per_token(x_ref, h_ref, z_ref, lbl_ref, hgn_ref, br_ref, sig_ref, wbr_ref, wout_ref,
                         st_ref, st_save_ref, *, tile):
    _hgrn2_per_token(z_ref, lbl_ref, st_ref, st_save_ref, tile=tile)
    n_rows = min(tile, RECURRENCE_CHUNK)
    hgn = hgn_ref[...]

    def block(r, carry):
        rows = pl.ds(pl.multiple_of(r * n_rows, n_rows), n_rows)
        gate = z_ref[rows, _HG:_HG + WIDTH]
        gate = gate * _sigmoid(gate)
        for hs in _HEAD_COLS:
            o = z_ref[rows, _HQ + hs.start:_HQ + hs.stop]
            br_ref[rows, hs] = (_rms(o, hgn[:, hs]) * gate[:, hs]).astype(BF16)
        _drain(_merge(x_ref, h_ref, br_ref, sig_ref, wbr_ref, wout_ref, rows=rows))
        return carry

    lax.fori_loop(0, tile // n_rows, block, 0)


def _mlstm(z_ref, gcol_ref, grow_ref, mln_ref, br_ref, c_ref, n_ref, m_ref, *, tile, length):
    L = length
    chunks = _chunks(tile, L)
    nc = len(chunks)
    causal = _lower_tri(L)
    tri = jnp.where(causal, 1.0, 0.0).astype(BF16)
    tri_t = jnp.where(lax.broadcasted_iota(jnp.int32, (L, L), 0) <= lax.broadcasted_iota(jnp.int32, (L, L), 1),
                      1.0, 0.0).astype(BF16)
    lane = lax.broadcasted_iota(jnp.int32, (L, HEAD_DIM), 1)
    sub = lax.broadcasted_iota(jnp.int32, (grow_ref.shape[0], L), 0)

    g_c, g_r, b_c, b_r = [], [], [], []
    for rows in chunks:
        gc = gcol_ref[rows, :]
        gr = grow_ref[:, rows]
        fc_hi, fc_lo = _split_bf16(jnp.where(lane >= HEADS, _log_sigmoid(gc), 0.0))
        fr_hi, fr_lo = _split_bf16(jnp.where(sub >= HEADS, _log_sigmoid(gr), 0.0))
        g_c.append(gc)
        g_r.append(gr)
        b_c.append(_dot(tri, fc_hi) + _dot(tri, fc_lo))
        b_r.append(_dot(fr_hi, tri_t) + _dot(fr_lo, tri_t))
    yield

    blk = [[None] * HEADS for _ in range(nc)]
    for c, rows in enumerate(chunks):
        for h in range(HEADS):
            cols = slice(h * HEAD_DIM, (h + 1) * HEAD_DIM)
            i_col = g_c[c][:, h:h + 1]
            i_row = g_r[c][h:h + 1, :]
            b_col = b_c[c][:, HEADS + h:HEADS + h + 1]
            b_row = b_r[c][HEADS + h:HEADS + h + 1, :]
            b_last = b_row[:, L - 1:L]
            log_w = jnp.where(causal, b_col - b_row + i_row, -jnp.inf)
            q = z_ref[rows, _MQ + cols.start:_MQ + cols.stop]
            k = z_ref[rows, _MK + cols.start:_MK + cols.stop] * (HEAD_DIM ** -0.5)
            qb = q.astype(BF16)
            blk[c][h] = dict(
                b_col=b_col, b_last=b_last, log_w=log_w, q=q, k=k, qb=qb,
                w_max=jnp.max(log_w, axis=-1, keepdims=True),
                last_max=jnp.max(b_last - b_row + i_row, axis=-1, keepdims=True),
                log_last=b_last - b_col + i_col,
                qk=_dot_nt(qb, k.astype(BF16)),
                vb=z_ref[rows, _MV + cols.start:_MV + cols.stop].astype(BF16))
            if h % 2 == 1:
                yield

    for h in range(HEADS):
        m = m_ref[h][:, 0:1]
        for c in range(nc):
            d = blk[c][h]
            d["m_in"] = m
            m = jnp.maximum(d["b_last"] + m, d["last_max"])
            d["m_out"] = m
        m_ref[h] = jnp.broadcast_to(m, (1, HEAD_DIM))
    yield

    for c in range(nc):
        for h in range(HEADS):
            d = blk[c][h]
            log_inter = d["b_col"] + d["m_in"]
            m_t = jnp.maximum(log_inter, d["w_max"])
            d["s"] = d["qk"] * jnp.exp(d["log_w"] - m_t)
            d["a"] = jnp.exp(log_inter - m_t)
            d["floor"] = jnp.exp(-m_t)
            kw = d["k"] * jnp.exp(d["log_last"] - d["m_out"])
            d["dec"] = jnp.exp(d["b_last"] + d["m_in"] - d["m_out"])
            d["c_upd"] = _dot_tn(kw.astype(BF16), d["vb"])
            d["n_upd"] = jnp.sum(kw, axis=0, keepdims=True)
            if h % 2 == 1:
                yield

    for h in range(HEADS):
        cm = c_ref[h]
        n = n_ref[h]
        for c in range(nc):
            d = blk[c][h]
            d["c_in"] = cm.astype(BF16)
            d["n_in"] = n
            cm = d["dec"] * cm + d["c_upd"]
            n = d["dec"] * n + d["n_upd"]
        c_ref[h] = cm
        n_ref[h] = n
    yield

    mln = mln_ref[...]
    for c, rows in enumerate(chunks):
        for h, hs in enumerate(_HEAD_COLS):
            d = blk[c][h]
            num = _dot(d["s"].astype(BF16), d["vb"]) + d["a"] * _dot(d["qb"], d["c_in"])
            den = (jnp.sum(d["s"], axis=-1, keepdims=True)
                   + d["a"] * jnp.sum(d["q"] * d["n_in"], axis=-1, keepdims=True))
            hval = num / jnp.maximum(jnp.abs(den), d["floor"])
            og = z_ref[rows, _MO + hs.start:_MO + hs.stop]
            br_ref[rows, WIDTH + hs.start:WIDTH + hs.stop] = (
                _rms(hval, mln[:, hs]) * _sigmoid(og)).astype(BF16)
            if h % 2 == 1:
                yield


def _xattn(z_ref, mk_ref, mv_ref, br_ref):
    lg = [_dot_nt(z_ref[:, _XQ + hs.start:_XQ + hs.stop].astype(BF16), mk_ref[:, hs]) * (HEAD_DIM ** -0.5)
          for hs in _HEAD_COLS]
    yield
    p = []
    for h in range(HEADS):
        e = jnp.exp(lg[h] - jnp.max(lg[h], axis=-1, keepdims=True))
        p.append((e / jnp.sum(e, axis=-1, keepdims=True)).astype(BF16))
        yield
    for h, hs in enumerate(_HEAD_COLS):
        br_ref[:, 2 * WIDTH + hs.start:2 * WIDTH + hs.stop] = _dot(p[h], mv_ref[:, hs]).astype(BF16)
    yield


def _merge(x_ref, h_ref, br_ref, sig_ref, wbr_ref, wout_ref, rows=slice(None)):
    d_model = x_ref.shape[-1]
    merged = None
    for i in range(N_BRANCH):
        y = (sig_ref[rows, i * d_model:(i + 1) * d_model]
             * _dot(br_ref[rows, i * WIDTH:(i + 1) * WIDTH], wbr_ref[i]))
        merged = y if merged is None else merged + y
        yield
    h_ref[rows, :] = x_ref[rows, :] + _dot(merged.astype(BF16), wout_ref[...])
    yield


_DONE = object()


def _rotate(*streams):
    lead = streams[0][0]
    while True:
        for gen, per_round in streams:
            for _ in range(per_round):
                if next(gen, _DONE) is _DONE and gen is lead:
                    return


def _drain(*gens):
    live = list(gens)
    while live:
        live = [gen for gen in live if next(gen, _DONE) is not _DONE]


def _mixer_kernel(xpair_ref, xnext_ref, s0_ref, c0_ref, n0_ref, m0_ref, mk_ref, mv_ref,
                  norm1_ref, wmain_ref, bmain_ref, wg_ref, wgt_ref, bgrow_ref, bgcol_ref,
                  lbl_ref, hgn_ref, mln_ref, wbr_ref, wout_ref,
                  h_ref, s_out_ref, c_out_ref, n_out_ref, m_out_ref,
                  uba_ref, ubb_ref, za_ref, zb_ref, siga_ref, sigb_ref, gca_ref, gcb_ref, gra_ref, grb_ref,
                  br_ref, st_ref, st_save_ref, c_ref, n_ref, m_ref,
                  *, tile, hg_chunk, ml_chunk, tiles_per_row):
    g = pl.program_id(0)
    proj_w = (wmain_ref, bmain_ref, wg_ref, wgt_ref, bgrow_ref, bgcol_ref)
    ubs = (uba_ref, ubb_ref)
    bufs = ((za_ref, siga_ref, gca_ref, gra_ref), (zb_ref, sigb_ref, gcb_ref, grb_ref))

    @pl.when(g == 0)
    def _prologue():
        _drain(_normalise(xpair_ref.at[0], norm1_ref, ubs[0]))
        _drain(_project(ubs[0], *proj_w, *bufs[0]))
        _drain(_normalise(xpair_ref.at[1], norm1_ref, ubs[1]))

    for i in range(2):
        tile_in_row = (2 * g + i) % tiles_per_row if tiles_per_row > 1 else 0
        r = i if tiles_per_row == 1 else 0

        def _load_state(r=r):
            for h in range(HEADS):
                st_ref[h] = s0_ref[r, h].T
                c_ref[h] = c0_ref[r, h]
                n_ref[h] = n0_ref[r, h:h + 1, :]
                m_ref[h] = m0_ref[r, h:h + 1, :]

        def _store_state(r=r):
            for h in range(HEADS):
                s_out_ref[r, h] = st_ref[h].T
                c_out_ref[r, h] = c_ref[h]
                n_out_ref[r, h:h + 1, :] = n_ref[h]
                m_out_ref[r, h:h + 1, :] = m_ref[h]

        if tiles_per_row == 1:
            _load_state()
        elif i == 0:
            pl.when(tile_in_row == 0)(_load_state)

        z_ref, sig_ref, gcol_ref, grow_ref = bufs[i]
        project_next = _project(ubs[1 - i], *proj_w, *bufs[1 - i])
        normalise_after = _normalise(xnext_ref.at[i], norm1_ref, ubs[i])
        min_decay = []
        hgrn2 = _hgrn2(z_ref, lbl_ref, hgn_ref, br_ref, st_ref, st_save_ref, min_decay,
                       tile=tile, length=hg_chunk)
        mlstm = _mlstm(z_ref, gcol_ref, grow_ref, mln_ref, br_ref, c_ref, n_ref, m_ref,
                       tile=tile, length=ml_chunk)
        xattn = _xattn(z_ref, mk_ref.at[r], mv_ref.at[r], br_ref)
        _rotate((hgrn2, 1), (mlstm, 1), (xattn, 1), (project_next, 1))
        _drain(mlstm, xattn)
        _rotate((_merge(xpair_ref.at[i], h_ref.at[i], br_ref, sig_ref, wbr_ref, wout_ref), 1),
                (project_next, 3), (normalise_after, 1))
        _drain(project_next, normalise_after)

        @pl.when(min_decay[0] < -HGRN2_LOG_DECAY_RANGE)
        def _redo_hgrn2(i=i, z_ref=z_ref):
            _redo_tile_per_token(xpair_ref.at[i], h_ref.at[i], z_ref, lbl_ref, hgn_ref, br_ref, sig_ref,
                                 wbr_ref, wout_ref, st_ref, st_save_ref, tile=tile)

        if tiles_per_row == 1:
            _store_state()
        elif i == 1:
            pl.when(tile_in_row == tiles_per_row - 1)(_store_state)


def _mixer(x, s0, c0, n0, m0, mk, mv, p):
    B, T, D = x.shape
    tile = TOKEN_TILE if T % TOKEN_TILE == 0 else T
    hg_chunk = RECURRENCE_CHUNK if tile % RECURRENCE_CHUNK == 0 else tile
    ml_chunk = MLSTM_CHUNK if tile % MLSTM_CHUNK == 0 else tile
    tiles_per_row = T // tile
    n_tiles = B * tiles_per_row
    assert n_tiles % 2 == 0 and (tiles_per_row == 1 or tiles_per_row % 2 == 0)
    rows = 2 if tiles_per_row == 1 else 1
    steps_per_row_block = 1 if tiles_per_row == 1 else tiles_per_row // 2
    n_main = p["w_main"].shape[1]
    n_slots = p["lb_logits"].shape[0]
    M = mk.shape[1]
    gate_rows = p["w_gate_t"].shape[0]

    def full(shape):
        return pl.BlockSpec(shape, lambda g: (0,) * len(shape), pipeline_mode=pl.Buffered(1))

    def per_row(shape):
        return pl.BlockSpec((rows,) + shape, lambda g: (g // steps_per_row_block,) + (0,) * len(shape))

    state = per_row((HEADS, HEAD_DIM, HEAD_DIM))
    vec = per_row((HEADS, HEAD_DIM))
    m0_rep = jnp.broadcast_to(m0[:, :, None], (B, HEADS, HEAD_DIM))
    xt = x.reshape(n_tiles, tile, D)
    kern = functools.partial(_mixer_kernel, tile=tile, hg_chunk=hg_chunk, ml_chunk=ml_chunk,
                             tiles_per_row=tiles_per_row)
    outs = pl.pallas_call(
        kern,
        grid=(n_tiles // 2,),
        in_specs=[
            pl.BlockSpec((2, tile, D), lambda g: (g, 0, 0)),
            pl.BlockSpec((2, tile, D), lambda g: (jnp.minimum(g + 1, n_tiles // 2 - 1), 0, 0)),
            state, state, vec, vec, per_row((M, WIDTH)), per_row((M, WIDTH)),
            full((1, D)), full((D, n_main)), full((1, n_main)),
            full((D, HEAD_DIM)), full((gate_rows, D)), full((1, HEAD_DIM)), full((gate_rows, 1)),
            full((n_slots, WIDTH)), full((1, WIDTH)), full((1, WIDTH)),
            full((N_BRANCH, WIDTH, D)), full((D, D)),
        ],
        out_specs=[pl.BlockSpec((2, tile, D), lambda g: (g, 0, 0)), state, state, vec, vec],
        out_shape=[
            jax.ShapeDtypeStruct((n_tiles, tile, D), F32),
            jax.ShapeDtypeStruct((B, HEADS, HEAD_DIM, HEAD_DIM), F32),
            jax.ShapeDtypeStruct((B, HEADS, HEAD_DIM, HEAD_DIM), F32),
            jax.ShapeDtypeStruct((B, HEADS, HEAD_DIM), F32),
            jax.ShapeDtypeStruct((B, HEADS, HEAD_DIM), F32),
        ],
        scratch_shapes=[
            pltpu.VMEM((tile, D), BF16), pltpu.VMEM((tile, D), BF16),
            pltpu.VMEM((tile, _GL), F32), pltpu.VMEM((tile, _GL), F32),
            pltpu.VMEM((tile, n_main - _GL), BF16), pltpu.VMEM((tile, n_main - _GL), BF16),
            pltpu.VMEM((tile, HEAD_DIM), F32), pltpu.VMEM((tile, HEAD_DIM), F32),
            pltpu.VMEM((gate_rows, tile), F32), pltpu.VMEM((gate_rows, tile), F32),
            pltpu.VMEM((tile, N_BRANCH * WIDTH), BF16),
            pltpu.VMEM((HEADS, HEAD_DIM, HEAD_DIM), F32),
            pltpu.VMEM((HEADS, HEAD_DIM, HEAD_DIM), F32),
            pltpu.VMEM((HEADS, HEAD_DIM, HEAD_DIM), F32),
            pltpu.VMEM((HEADS, 1, HEAD_DIM), F32),
            pltpu.VMEM((HEADS, 1, HEAD_DIM), F32),
        ],
        compiler_params=pltpu.CompilerParams(
            dimension_semantics=("arbitrary",), vmem_limit_bytes=VMEM_LIMIT_BYTES),
        name="mixer",
    )(xt, xt, s0, c0, n0, m0_rep, mk, mv,
      p["norm1"], p["w_main"], p["b_main"], p["w_gate"], p["w_gate_t"], p["b_gate_row"], p["b_gate_col"],
      p["lb_logits"], p["hg_norm"], p["ml_norm"], p["w_branch"], p["w_out"])
    return (outs[0].reshape(B, T, D),) + tuple(outs[1:])


_GELU_C = 0.7978845608028654


def _gelu_tanh_times(x, y):
    inner = x * (_GELU_C + (_GELU_C * 0.044715) * (x * x))
    return (0.5 + 0.5 * jnp.tanh(inner)) * (x * y)


def _ffn_kernel(h_ref, cv0_ref, norm2_ref, wup_ref, cw_ref, cb_ref, wdn_ref, fn_ref,
                y_ref, cv_out_ref, ug_ref, act_ref, *, tile, d_ff):
    t = pl.program_id(1)
    last_t = pl.num_programs(1) - 1
    head = 8
    keep = CONV_W - 1

    @pl.when(t == 0)
    def _first():
        ug_ref[0:head - keep, :] = jnp.zeros((head - keep, d_ff), F32)
        ug_ref[head - keep:head, :] = cv0_ref[0]

    @pl.when(t > 0)
    def _carry():
        ug_ref[head - keep:head, :] = ug_ref[head + tile - keep:head + tile, :]

    h = h_ref[0]
    hb = _rms(h, norm2_ref[...]).astype(BF16)
    for c0 in range(0, d_ff, 256):
        cols = slice(c0, c0 + 256)
        ug_ref[head:head + tile, cols] = _dot(hb, wup_ref[:, cols])
        up_v = _dot(hb, wup_ref[:, d_ff + c0:d_ff + c0 + 256])
        rows = ug_ref[:, cols]
        conv = cb_ref[:, cols] + cw_ref[CONV_W - 1:CONV_W, cols] * rows[head:, :]
        for j in range(1, CONV_W):
            conv = conv + cw_ref[CONV_W - 1 - j:CONV_W - j, cols] * pltpu.roll(rows, j, axis=0)[head:, :]
        act_ref[:, cols] = _gelu_tanh_times(conv, up_v).astype(BF16)
    h2 = h + _dot(act_ref[...], wdn_ref[...])
    y_ref[0] = _rms(h2, fn_ref[...])

    @pl.when(t == last_t)
    def _store():
        cv_out_ref[0] = ug_ref[head + tile - keep:head + tile, :]


def _ffn(h, cv0, p):
    B, T, D = h.shape
    d_ff = p["w_down"].shape[0]
    tile = FFN_TOKEN_TILE if T % FFN_TOKEN_TILE == 0 else T
    keep = CONV_W - 1

    def full(shape):
        return pl.BlockSpec(shape, lambda b, t: (0,) * len(shape), pipeline_mode=pl.Buffered(1))

    kern = functools.partial(_ffn_kernel, tile=tile, d_ff=d_ff)
    return pl.pallas_call(
        kern,
        grid=(B, T // tile),
        in_specs=[
            pl.BlockSpec((1, tile, D), lambda b, t: (b, t, 0)),
            pl.BlockSpec((1, keep, d_ff), lambda b, t: (b, 0, 0)),
            full((1, D)), full((D, 2 * d_ff)), full((CONV_W, d_ff)), full((1, d_ff)),
            full((d_ff, D)), full((1, D)),
        ],
        out_specs=[pl.BlockSpec((1, tile, D), lambda b, t: (b, t, 0)),
                   pl.BlockSpec((1, keep, d_ff), lambda b, t: (b, 0, 0))],
        out_shape=[jax.ShapeDtypeStruct((B, T, D), F32),
                   jax.ShapeDtypeStruct((B, keep, d_ff), F32)],
        scratch_shapes=[pltpu.VMEM((8 + tile, d_ff), F32),
                        pltpu.VMEM((tile, d_ff), BF16)],
        compiler_params=pltpu.CompilerParams(
            dimension_semantics=("arbitrary", "arbitrary"), vmem_limit_bytes=VMEM_LIMIT_BYTES),
        name="conv_ffn",
    )(h, cv0, p["norm2"], p["w_up"], p["conv_w"], p["conv_b"], p["w_down"], p["final_norm"])


def _prepare(norm1, w_in, b_in, ml_fgate_bias, hg_lb_logits, hg_norm, ml_norm, w_branch, w_out,
             norm2, w_up, ffn_conv_w, ffn_conv_b, w_down, final_norm):
    D = w_in.shape[0]
    g0 = 8 * WIDTH
    g1 = g0 + 2 * HEADS
    w_main = jnp.concatenate([w_in[:, :g0].astype(BF16), w_in[:, g1:].astype(BF16)], axis=1)
    b_main = jnp.concatenate([b_in[:g0], b_in[g1:]])[None, :]
    w_gate = jnp.zeros((D, HEAD_DIM), F32).at[:, :2 * HEADS].set(w_in[:, g0:g1]).astype(BF16)
    gate_rows = 16
    w_gate_t = jnp.zeros((gate_rows, D), F32).at[:2 * HEADS].set(w_in[:, g0:g1].T).astype(BF16)
    b_gate = b_in[g0:g1] + jnp.concatenate([jnp.zeros((HEADS,), F32), ml_fgate_bias])
    return dict(
        norm1=norm1[None, :], w_main=w_main, b_main=b_main, w_gate=w_gate, w_gate_t=w_gate_t,
        b_gate_row=jnp.zeros((1, HEAD_DIM), F32).at[0, :2 * HEADS].set(b_gate),
        b_gate_col=jnp.zeros((gate_rows, 1), F32).at[:2 * HEADS, 0].set(b_gate),
        lb_logits=hg_lb_logits, hg_norm=hg_norm[None, :], ml_norm=ml_norm[None, :],
        w_branch=w_branch.astype(BF16), w_out=w_out.astype(BF16),
        norm2=norm2[None, :], w_up=w_up.astype(BF16), conv_w=ffn_conv_w, conv_b=ffn_conv_b[None, :],
        w_down=w_down.astype(BF16), final_norm=final_norm[None, :])


def _layer(x, s_hg, s_c, s_n, s_m, s_cv, mk, mv, p):
    h1, hg, c, n, m = _mixer(x, s_hg, s_c, s_n, s_m, mk, mv, p)
    y, cv = _ffn(h1, s_cv, p)
    return y, hg, c, n, m[:, :, 0], cv


def kernel(x_prompt, x_sample, state_hgrn, state_mlstm_C, state_mlstm_n, state_mlstm_m, state_ffn_conv, cache_mem_k, cache_mem_v, mem_prompt, norm1, w_in, b_in, ml_fgate_bias, hg_lb_logits, hg_norm, ml_norm, mem_norm, w_mem_kv, w_branch, w_out, norm2, w_up, ffn_conv_w, ffn_conv_b, w_down, final_norm):
    depth = norm1.shape[0]
    assert depth == 1, "single-layer encoder"
    assert hg_lb_logits.shape[0] == depth + 1
    Bp = x_prompt.shape[0]
    Bs = x_sample.shape[0]
    M = mem_prompt.shape[1]
    d_ff = w_down.shape[1]
    l = 0
    p = _prepare(norm1[l], w_in[l], b_in[l], ml_fgate_bias[l], hg_lb_logits, hg_norm[l], ml_norm[l],
                 w_branch[l], w_out[l], norm2[l], w_up[l], ffn_conv_w[l], ffn_conv_b[l], w_down[l], final_norm)

    mk_p, mv_p, mk_pb, mv_pb = _memory_kv(mem_prompt, mem_norm[l], w_mem_kv[l])
    zeros = lambda *s: jnp.zeros(s, F32)
    yp, hg_p, c_p, n_p, m_p, cv_p = _layer(
        x_prompt, zeros(Bp, HEADS, HEAD_DIM, HEAD_DIM), zeros(Bp, HEADS, HEAD_DIM, HEAD_DIM),
        zeros(Bp, HEADS, HEAD_DIM), zeros(Bp, HEADS), zeros(Bp, CONV_W - 1, d_ff), mk_pb, mv_pb, p)
    ys, hg_s, c_s, n_s, m_s, cv_s = _layer(
        x_sample, state_hgrn[l], state_mlstm_C[l], state_mlstm_n[l], state_mlstm_m[l], state_ffn_conv[l],
        cache_mem_k[l].reshape(Bs, M, WIDTH).astype(BF16), cache_mem_v[l].reshape(Bs, M, WIDTH).astype(BF16), p)

    return (yp, ys,
            hg_p[None], c_p[None], n_p[None], m_p[None],
            mk_p[None], mv_p[None], cv_p[None],
            hg_s[None], c_s[None], n_s[None], m_s[None], cv_s[None])
```

```python
import functools

import jax
import jax.numpy as jnp
from jax import lax
from jax.experimental import pallas as pl
from jax.experimental.pallas import tpu as pltpu

HEADS = 4
HEAD_DIM = 128
WIDTH = HEADS * HEAD_DIM
N_BRANCH = 3
CONV_W = 3
NORM_EPS = 1e-6
RECURRENCE_CHUNK = 64
MLSTM_CHUNK = 128
TOKEN_TILE = 256
FFN_TOKEN_TILE = 1024
COL_GROUP = 256
NORM_ROWS = 64
VMEM_LIMIT_BYTES = 60 * 1024 * 1024
HGRN2_LOG_DECAY_RANGE = 60.0

F32 = jnp.float32
BF16 = jnp.bfloat16

_HQ, _HF, _HI, _HG = 0, WIDTH, 2 * WIDTH, 3 * WIDTH
_MQ, _MK, _MV, _MO = 4 * WIDTH, 5 * WIDTH, 6 * WIDTH, 7 * WIDTH
_XQ = 8 * WIDTH
_GL = 9 * WIDTH

_HEAD_COLS = tuple(slice(h * HEAD_DIM, (h + 1) * HEAD_DIM) for h in range(HEADS))


def _dot(a, b):
    return jnp.dot(a, b, preferred_element_type=F32)


def _dot_nt(a, b):
    return lax.dot_general(a, b, (((1,), (1,)), ((), ())), preferred_element_type=F32)


def _dot_tn(a, b):
    return lax.dot_general(a, b, (((0,), (0,)), ((), ())), preferred_element_type=F32)


def _sigmoid(x):
    return 1.0 / (1.0 + jnp.exp(-x))


def _log_sigmoid(x):
    return jnp.minimum(x, 0.0) - jnp.log(1.0 + jnp.exp(-jnp.abs(x)))


def _rms(x, g):
    return x * lax.rsqrt(jnp.mean(x * x, axis=-1, keepdims=True) + NORM_EPS) * g


def _split_bf16(x):
    hi = x.astype(BF16)
    lo = (x - hi.astype(F32)).astype(BF16)
    return hi, lo


def _lower_tri(n):
    r = lax.broadcasted_iota(jnp.int32, (n, n), 0)
    c = lax.broadcasted_iota(jnp.int32, (n, n), 1)
    return r >= c


def _chunks(tile, length):
    return [slice(c * length, (c + 1) * length) for c in range(tile // length)]


def _memkv_kernel(mem_ref, g_ref, w_ref, k_ref, v_ref, kb_ref, vb_ref):
    u = _rms(mem_ref[0], g_ref[...]).astype(BF16)
    kv = _dot(u, w_ref[...])
    k = kv[:, :WIDTH]
    v = kv[:, WIDTH:]
    for h, hs in enumerate(_HEAD_COLS):
        k_ref[0, :, h, :] = k[:, hs]
        v_ref[0, :, h, :] = v[:, hs]
    kb_ref[0] = k.astype(BF16)
    vb_ref[0] = v.astype(BF16)


def _memory_kv(mem, g, w):
    B, M, D = mem.shape
    full = lambda shape: pl.BlockSpec(shape, lambda b: (0,) * len(shape))
    per_b = lambda shape: pl.BlockSpec((1,) + shape, lambda b: (b, 0, 0))
    return pl.pallas_call(
        _memkv_kernel,
        grid=(B,),
        in_specs=[per_b((M, D)), full((1, D)), full((D, 2 * WIDTH))],
        out_specs=[pl.BlockSpec((1, M, HEADS, HEAD_DIM), lambda b: (b, 0, 0, 0))] * 2 + [per_b((M, WIDTH))] * 2,
        out_shape=[jax.ShapeDtypeStruct((B, M, HEADS, HEAD_DIM), F32)] * 2
        + [jax.ShapeDtypeStruct((B, M, WIDTH), BF16)] * 2,
        compiler_params=pltpu.CompilerParams(dimension_semantics=("arbitrary",)),
        name="memory_kv",
    )(mem, g.reshape(1, D), w.astype(BF16))


def _normalise(x_ref, norm1_ref, ub_ref):
    n_rows = min(x_ref.shape[0], NORM_ROWS)
    for r0 in range(0, x_ref.shape[0], n_rows):
        rows = slice(r0, r0 + n_rows)
        ub_ref[rows, :] = _rms(x_ref[rows, :], norm1_ref[...]).astype(BF16)
        yield


def _project(ub_ref, wmain_ref, bmain_ref, wg_ref, wgt_ref, bgrow_ref, bgcol_ref,
             z_ref, sig_ref, gcol_ref, grow_ref):
    ub = ub_ref[...]
    n_main = wmain_ref.shape[1]
    gcol_ref[...] = _dot(ub, wg_ref[...]) + bgrow_ref[...]
    grow_ref[...] = _dot_nt(wgt_ref[...], ub) + bgcol_ref[...]
    for c0 in range(0, n_main, COL_GROUP):
        cols = slice(c0, c0 + COL_GROUP)
        zc = _dot(ub, wmain_ref[:, cols]) + bmain_ref[:, cols]
        if c0 < _GL:
            z_ref[:, cols] = zc
        else:
            sig_ref[:, c0 - _GL:c0 - _GL + COL_GROUP] = _sigmoid(zc).astype(BF16)
        yield


def _forget_floor(lbl_ref):
    logits = lbl_ref[...]
    e = jnp.exp(logits - jnp.max(logits, axis=0, keepdims=True))
    return e[0:1, :] / jnp.sum(e, axis=0, keepdims=True)


def _hgrn2(z_ref, lbl_ref, hgn_ref, br_ref, st_ref, st_save_ref, min_decay, *, tile, length, seg_len):
    L = length
    chunks = _chunks(tile, L)
    blocks_per_seg = seg_len // L
    lb = _forget_floor(lbl_ref)
    causal = _lower_tri(L)
    tri = jnp.where(causal, 1.0, 0.0).astype(BF16)
    for sg in range(tile // seg_len):
        for h in range(HEADS):
            st_save_ref[sg, h] = st_ref[sg, h]

    k, a_cum = [], []
    for rows in chunks:
        f = lb + (1.0 - lb) * _sigmoid(z_ref[rows, _HF:_HF + WIDTH])
        g_hi, g_lo = _split_bf16(jnp.log(f))
        a_cum.append(_dot(tri, g_hi) + _dot(tri, g_lo))
        k.append(1.0 - f)
        yield

    q_rel, k_rel, q_in, k_out, decay = [], [], [], [], []
    for c, rows in enumerate(chunks):
        a = a_cum[c]
        a_mid = a[L // 2 - 1:L // 2, :]
        a_end = a[L - 1:L, :]
        hq = z_ref[rows, _HQ:_HQ + WIDTH]
        qr = hq * _sigmoid(hq) * jnp.exp(a - a_mid)
        kr = k[c] * jnp.exp(a_mid - a)
        q_in.append((qr * jnp.exp(a_mid)).astype(BF16))
        k_out.append((kr * jnp.exp(a_end - a_mid)).astype(BF16))
        q_rel.append(qr.astype(BF16))
        k_rel.append(kr.astype(BF16))
        decay.append(jnp.exp(a_end))
        yield
    ends = a_cum[0][L - 1:L, :]
    for a in a_cum[1:]:
        ends = jnp.minimum(ends, a[L - 1:L, :])
    min_decay.append(jnp.min(ends))
    yield

    v = [z_ref[rows, _HI:_HI + WIDTH].astype(BF16) for rows in chunks]
    scores = [[jnp.where(causal, _dot_nt(q_rel[c][:, hs], k_rel[c][:, hs]), 0.0).astype(BF16)
               for hs in _HEAD_COLS] for c in range(len(chunks))]
    update = [[_dot_tn(v[c][:, hs], k_out[c][:, hs]) for hs in _HEAD_COLS] for c in range(len(chunks))]
    yield

    s_in = []
    for h, hs in enumerate(_HEAD_COLS):
        per_chunk = []
        for sg in range(tile // seg_len):
            s = st_ref[sg, h]
            for c in range(sg * blocks_per_seg, (sg + 1) * blocks_per_seg):
                per_chunk.append(s.astype(BF16))
                s = decay[c][:, hs] * s + update[c][h]
            st_ref[sg, h] = s
        s_in.append(per_chunk)
        yield

    hgn = hgn_ref[...]
    for c, rows in enumerate(chunks):
        gate = z_ref[rows, _HG:_HG + WIDTH]
        gate = gate * _sigmoid(gate)
        for h, hs in enumerate(_HEAD_COLS):
            o = _dot(scores[c][h], v[c][:, hs]) + _dot_nt(q_in[c][:, hs], s_in[h][c])
            br_ref[rows, hs] = (_rms(o, hgn[:, hs]) * gate[:, hs]).astype(BF16)
        yield


def _hgrn2_per_token(z_ref, lbl_ref, st_ref, st_save_ref, *, seg, seg_len):
    lb = _forget_floor(lbl_ref)
    st_ref = st_ref.at[seg]
    for h in range(HEADS):
        st_ref[h] = st_save_ref[seg, h]

    pad = 16
    first_row = lax.broadcasted_iota(jnp.int32, (pad, WIDTH), 0) == 0
    group = 8
    row_id = lax.broadcasted_iota(jnp.int32, (group, WIDTH), 0)

    def token_group(gi, carry):
        rows = pl.ds(pl.multiple_of(seg * seg_len + gi * group, group), group)
        hq = z_ref[rows, _HQ:_HQ + WIDTH]
        f = lb + (1.0 - lb) * _sigmoid(z_ref[rows, _HF:_HF + WIDTH])
        q = hq * _sigmoid(hq)
        k = 1.0 - f
        v = z_ref[rows, _HI:_HI + WIDTH]
        out = jnp.zeros((group, WIDTH), F32)
        for j in range(group):
            qj = jnp.where(first_row, q[j:j + 1, :], 0.0).astype(BF16)
            kj = jnp.where(first_row, k[j:j + 1, :], 0.0).astype(BF16)
            vj = jnp.where(first_row, v[j:j + 1, :], 0.0).astype(BF16)
            outs = []
            for h, hs in enumerate(_HEAD_COLS):
                s = f[j:j + 1, hs] * st_ref[h] + _dot_tn(vj[:, hs], kj[:, hs])
                st_ref[h] = s
                outs.append(_dot_nt(qj[:, hs], s.astype(BF16))[0:1, :])
            out = jnp.where(row_id == j, jnp.concatenate(outs, axis=-1), out)
        z_ref[rows, _HQ:_HQ + WIDTH] = out
        return carry

    lax.fori_loop(0, seg_len // group, token_group, 0)


def _redo_tile_per_token(x_ref, h_ref, z_ref, lbl_ref, hgn_ref, br_ref, sig_ref, wbr_ref, wout_ref,
                         st_ref, st_save_ref, *, tile, seg_len):
    for seg in range(tile // seg_len):
        _hgrn2_per_token(z_ref, lbl_ref, st_ref, st_save_ref, seg=seg, seg_len=seg_len)
    n_rows = min(tile, RECURRENCE_CHUNK)
    hgn = hgn_ref[...]

    def block(r, carry):
        rows = pl.ds(pl.multiple_of(r * n_rows, n_rows), n_rows)
        gate = z_ref[rows, _HG:_HG + WIDTH]
        gate = gate * _sigmoid(gate)
        for hs in _HEAD_COLS:
            o = z_ref[rows, _HQ + hs.start:_HQ + hs.stop]
            br_ref[rows, hs] = (_rms(o, hgn[:, hs]) * gate[:, hs]).astype(BF16)
        _drain(_merge(x_ref, h_ref, br_ref, sig_ref, wbr_ref, wout_ref, rows=rows))
        return carry

    lax.fori_loop(0, tile // n_rows, block, 0)


def _mlstm(z_ref, gcol_ref, grow_ref, mln_ref, br_ref, c_ref, n_ref, m_ref, *, tile, length, seg_len):
    L = length
    chunks = _chunks(tile, L)
    nc = len(chunks)
    seg_blocks = [range(sg * (seg_len // L), (sg + 1) * (seg_len // L)) for sg in range(tile // seg_len)]
    causal = _lower_tri(L)
    tri = jnp.where(causal, 1.0, 0.0).astype(BF16)
    tri_t = jnp.where(lax.broadcasted_iota(jnp.int32, (L, L), 0) <= lax.broadcasted_iota(jnp.int32, (L, L), 1),
                      1.0, 0.0).astype(BF16)
    lane = lax.broadcasted_iota(jnp.int32, (L, HEAD_DIM), 1)
    sub = lax.broadcasted_iota(jnp.int32, (grow_ref.shape[0], L), 0)

    g_c, g_r, b_c, b_r = [], [], [], []
    for rows in chunks:
        gc = gcol_ref[rows, :]
        gr = grow_ref[:, rows]
        fc_hi, fc_lo = _split_bf16(jnp.where(lane >= HEADS, _log_sigmoid(gc), 0.0))
        fr_hi, fr_lo = _split_bf16(jnp.where(sub >= HEADS, _log_sigmoid(gr), 0.0))
        g_c.append(gc)
        g_r.append(gr)
        b_c.append(_dot(tri, fc_hi) + _dot(tri, fc_lo))
        b_r.append(_dot(fr_hi, tri_t) + _dot(fr_lo, tri_t))
    yield

    blk = [[None] * HEADS for _ in range(nc)]
    for c, rows in enumerate(chunks):
        for h in range(HEADS):
            cols = slice(h * HEAD_DIM, (h + 1) * HEAD_DIM)
            i_col = g_c[c][:, h:h + 1]
            i_row = g_r[c][h:h + 1, :]
            b_col = b_c[c][:, HEADS + h:HEADS + h + 1]
            b_row = b_r[c][HEADS + h:HEADS + h + 1, :]
            b_last = b_row[:, L - 1:L]
            log_w = jnp.where(causal, b_col - b_row + i_row, -jnp.inf)
            q = z_ref[rows, _MQ + cols.start:_MQ + cols.stop]
            k = z_ref[rows, _MK + cols.start:_MK + cols.stop] * (HEAD_DIM ** -0.5)
            qb = q.astype(BF16)
            blk[c][h] = dict(
                b_col=b_col, b_last=b_last, log_w=log_w, q=q, k=k, qb=qb,
                w_max=jnp.max(log_w, axis=-1, keepdims=True),
                last_max=jnp.max(b_last - b_row + i_row, axis=-1, keepdims=True),
                log_last=b_last - b_col + i_col,
                qk=_dot_nt(qb, k.astype(BF16)),
                vb=z_ref[rows, _MV + cols.start:_MV + cols.stop].astype(BF16))
            if h % 2 == 1:
                yield

    for h in range(HEADS):
        for sg, blocks in enumerate(seg_blocks):
            m = m_ref[sg, h][:, 0:1]
            for c in blocks:
                d = blk[c][h]
                d["m_in"] = m
                m = jnp.maximum(d["b_last"] + m, d["last_max"])
                d["m_out"] = m
            m_ref[sg, h] = jnp.broadcast_to(m, (1, HEAD_DIM))
    yield

    for c in range(nc):
        for h in range(HEADS):
            d = blk[c][h]
            log_inter = d["b_col"] + d["m_in"]
            m_t = jnp.maximum(log_inter, d["w_max"])
            d["s"] = d["qk"] * jnp.exp(d["log_w"] - m_t)
            d["a"] = jnp.exp(log_inter - m_t)
            d["floor"] = jnp.exp(-m_t)
            kw = d["k"] * jnp.exp(d["log_last"] - d["m_out"])
            d["dec"] = jnp.exp(d["b_last"] + d["m_in"] - d["m_out"])
            d["c_upd"] = _dot_tn(kw.astype(BF16), d["vb"])
            d["n_upd"] = jnp.sum(kw, axis=0, keepdims=True)
            if h % 2 == 1:
                yield

    for h in range(HEADS):
        for sg, blocks in enumerate(seg_blocks):
            cm = c_ref[sg, h]
            n = n_ref[sg, h]
            for c in blocks:
                d = blk[c][h]
                d["c_in"] = cm.astype(BF16)
                d["n_in"] = n
                cm = d["dec"] * cm + d["c_upd"]
                n = d["dec"] * n + d["n_upd"]
            c_ref[sg, h] = cm
            n_ref[sg, h] = n
    yield

    mln = mln_ref[...]
    for c, rows in enumerate(chunks):
        for h, hs in enumerate(_HEAD_COLS):
            d = blk[c][h]
            num = _dot(d["s"].astype(BF16), d["vb"]) + d["a"] * _dot(d["qb"], d["c_in"])
            den = (jnp.sum(d["s"], axis=-1, keepdims=True)
                   + d["a"] * jnp.sum(d["q"] * d["n_in"], axis=-1, keepdims=True))
            hval = num / jnp.maximum(jnp.abs(den), d["floor"])
            og = z_ref[rows, _MO + hs.start:_MO + hs.stop]
            br_ref[rows, WIDTH + hs.start:WIDTH + hs.stop] = (
                _rms(hval, mln[:, hs]) * _sigmoid(og)).astype(BF16)
            if h % 2 == 1:
                yield


def _xattn(z_ref, mk_ref, mv_ref, br_ref, *, tile, seg_len):
    segs = _chunks(tile, seg_len)
    lg = [[_dot_nt(z_ref[rows, _XQ + hs.start:_XQ + hs.stop].astype(BF16), mk_ref[sg, :, hs]) * (HEAD_DIM ** -0.5)
           for sg, rows in enumerate(segs)] for hs in _HEAD_COLS]
    yield
    p = []
    for h in range(HEADS):
        per_seg = []
        for sg in range(len(segs)):
            e = jnp.exp(lg[h][sg] - jnp.max(lg[h][sg], axis=-1, keepdims=True))
            per_seg.append((e / jnp.sum(e, axis=-1, keepdims=True)).astype(BF16))
        p.append(per_seg)
        yield
    for h, hs in enumerate(_HEAD_COLS):
        for sg, rows in enumerate(segs):
            br_ref[rows, 2 * WIDTH + hs.start:2 * WIDTH + hs.stop] = _dot(
                p[h][sg], mv_ref[sg, :, hs]).astype(BF16)
    yield


def _merge(x_ref, h_ref, br_ref, sig_ref, wbr_ref, wout_ref, rows=slice(None)):
    d_model = x_ref.shape[-1]
    merged = None
    for i in range(N_BRANCH):
        y = (sig_ref[rows, i * d_model:(i + 1) * d_model]
             * _dot(br_ref[rows, i * WIDTH:(i + 1) * WIDTH], wbr_ref[i]))
        merged = y if merged is None else merged + y
        yield
    h_ref[rows, :] = x_ref[rows, :] + _dot(merged.astype(BF16), wout_ref[...])
    yield


_DONE = object()


def _rotate(*streams):
    lead = streams[0][0]
    while True:
        for gen, per_round in streams:
            for _ in range(per_round):
                if next(gen, _DONE) is _DONE and gen is lead:
                    return


def _drain(*gens):
    live = list(gens)
    while live:
        live = [gen for gen in live if next(gen, _DONE) is not _DONE]


def _mixer_kernel(xpair_ref, xnext_ref, s0_ref, c0_ref, n0_ref, m0_ref, mk_ref, mv_ref,
                  norm1_ref, wmain_ref, bmain_ref, wg_ref, wgt_ref, bgrow_ref, bgcol_ref,
                  lbl_ref, hgn_ref, mln_ref, wbr_ref, wout_ref,
                  h_ref, s_out_ref, c_out_ref, n_out_ref, m_out_ref,
                  uba_ref, ubb_ref, za_ref, zb_ref, siga_ref, sigb_ref, gca_ref, gcb_ref, gra_ref, grb_ref,
                  br_ref, st_ref, st_save_ref, c_ref, n_ref, m_ref,
                  *, tile, seg_len, hg_chunk, ml_chunk, tiles_per_row):
    g = pl.program_id(0)
    n_seg = tile // seg_len
    proj_w = (wmain_ref, bmain_ref, wg_ref, wgt_ref, bgrow_ref, bgcol_ref)
    ubs = (uba_ref, ubb_ref)
    bufs = ((za_ref, siga_ref, gca_ref, gra_ref), (zb_ref, sigb_ref, gcb_ref, grb_ref))

    @pl.when(g == 0)
    def _prologue():
        _drain(_normalise(xpair_ref.at[0], norm1_ref, ubs[0]))
        _drain(_project(ubs[0], *proj_w, *bufs[0]))
        _drain(_normalise(xpair_ref.at[1], norm1_ref, ubs[1]))

    for i in range(2):
        tile_in_row = (2 * g + i) % tiles_per_row if tiles_per_row > 1 else 0
        r0 = i * n_seg if tiles_per_row == 1 else 0

        def _load_state(r0=r0):
            for sg in range(n_seg):
                for h in range(HEADS):
                    st_ref[sg, h] = s0_ref[r0 + sg, h].T
                    c_ref[sg, h] = c0_ref[r0 + sg, h]
                    n_ref[sg, h] = n0_ref[r0 + sg, h:h + 1, :]
                    m_ref[sg, h] = m0_ref[r0 + sg, h:h + 1, :]

        def _store_state(r0=r0):
            for sg in range(n_seg):
                for h in range(HEADS):
                    s_out_ref[r0 + sg, h] = st_ref[sg, h].T
                    c_out_ref[r0 + sg, h] = c_ref[sg, h]
                    n_out_ref[r0 + sg, h:h + 1, :] = n_ref[sg, h]
                    m_out_ref[r0 + sg, h:h + 1, :] = m_ref[sg, h]

        if tiles_per_row == 1:
            _load_state()
        elif i == 0:
            pl.when(tile_in_row == 0)(_load_state)

        z_ref, sig_ref, gcol_ref, grow_ref = bufs[i]
        project_next = _project(ubs[1 - i], *proj_w, *bufs[1 - i])
        normalise_after = _normalise(xnext_ref.at[i], norm1_ref, ubs[i])
        min_decay = []
        hgrn2 = _hgrn2(z_ref, lbl_ref, hgn_ref, br_ref, st_ref, st_save_ref, min_decay,
                       tile=tile, length=hg_chunk, seg_len=seg_len)
        mlstm = _mlstm(z_ref, gcol_ref, grow_ref, mln_ref, br_ref, c_ref, n_ref, m_ref,
                       tile=tile, length=ml_chunk, seg_len=seg_len)
        xattn = _xattn(z_ref, mk_ref.at[r0:r0 + n_seg], mv_ref.at[r0:r0 + n_seg], br_ref,
                       tile=tile, seg_len=seg_len)
        _rotate((hgrn2, 1), (mlstm, 1), (xattn, 1), (project_next, 1))
        _drain(mlstm, xattn)
        _rotate((_merge(xpair_ref.at[i], h_ref.at[i], br_ref, sig_ref, wbr_ref, wout_ref), 1),
                (project_next, 3), (normalise_after, 1))
        _drain(project_next, normalise_after)

        @pl.when(min_decay[0] < -HGRN2_LOG_DECAY_RANGE)
        def _redo_hgrn2(i=i, z_ref=z_ref):
            _redo_tile_per_token(xpair_ref.at[i], h_ref.at[i], z_ref, lbl_ref, hgn_ref, br_ref, sig_ref,
                                 wbr_ref, wout_ref, st_ref, st_save_ref, tile=tile, seg_len=seg_len)

        if tiles_per_row == 1:
            _store_state()
        elif i == 1:
            pl.when(tile_in_row == tiles_per_row - 1)(_store_state)


def _mixer(x, s0, c0, n0, m0, mk, mv, p):
    B, T, D = x.shape
    if T % TOKEN_TILE == 0:
        tile = seg_len = TOKEN_TILE
    else:
        seg_len = T
        n_seg = max(n for n in range(1, B // 2 + 1) if B % (2 * n) == 0 and n * T <= TOKEN_TILE)
        tile = n_seg * T
    hg_chunk = RECURRENCE_CHUNK if seg_len % RECURRENCE_CHUNK == 0 else seg_len
    ml_chunk = MLSTM_CHUNK if seg_len % MLSTM_CHUNK == 0 else seg_len
    tiles_per_row = T // seg_len
    n_tiles = B * T // tile
    assert n_tiles % 2 == 0 and (tiles_per_row == 1 or tiles_per_row % 2 == 0)
    rows = 2 * (tile // seg_len) if tiles_per_row == 1 else 1
    steps_per_row_block = 1 if tiles_per_row == 1 else tiles_per_row // 2
    n_main = p["w_main"].shape[1]
    n_slots = p["lb_logits"].shape[0]
    M = mk.shape[1]
    gate_rows = p["w_gate_t"].shape[0]

    def full(shape):
        return pl.BlockSpec(shape, lambda g: (0,) * len(shape), pipeline_mode=pl.Buffered(1))

    def per_row(shape):
        return pl.BlockSpec((rows,) + shape, lambda g: (g // steps_per_row_block,) + (0,) * len(shape))

    state = per_row((HEADS, HEAD_DIM, HEAD_DIM))
    vec = per_row((HEADS, HEAD_DIM))
    m0_rep = jnp.broadcast_to(m0[:, :, None], (B, HEADS, HEAD_DIM))
    xt = x.reshape(n_tiles, tile, D)
    kern = functools.partial(_mixer_kernel, tile=tile, seg_len=seg_len, hg_chunk=hg_chunk, ml_chunk=ml_chunk,
                             tiles_per_row=tiles_per_row)
    n_seg = tile // seg_len
    outs = pl.pallas_call(
        kern,
        grid=(n_tiles // 2,),
        in_specs=[
            pl.BlockSpec((2, tile, D), lambda g: (g, 0, 0)),
            pl.BlockSpec((2, tile, D), lambda g: (jnp.minimum(g + 1, n_tiles // 2 - 1), 0, 0)),
            state, state, vec, vec, per_row((M, WIDTH)), per_row((M, WIDTH)),
            full((1, D)), full((D, n_main)), full((1, n_main)),
            full((D, HEAD_DIM)), full((gate_rows, D)), full((1, HEAD_DIM)), full((gate_rows, 1)),
            full((n_slots, WIDTH)), full((1, WIDTH)), full((1, WIDTH)),
            full((N_BRANCH, WIDTH, D)), full((D, D)),
        ],
        out_specs=[pl.BlockSpec((2, tile, D), lambda g: (g, 0, 0)), state, state, vec, vec],
        out_shape=[
            jax.ShapeDtypeStruct((n_tiles, tile, D), F32),
            jax.ShapeDtypeStruct((B, HEADS, HEAD_DIM, HEAD_DIM), F32),
            jax.ShapeDtypeStruct((B, HEADS, HEAD_DIM, HEAD_DIM), F32),
            jax.ShapeDtypeStruct((B, HEADS, HEAD_DIM), F32),
            jax.ShapeDtypeStruct((B, HEADS, HEAD_DIM), F32),
        ],
        scratch_shapes=[
            pltpu.VMEM((tile, D), BF16), pltpu.VMEM((tile, D), BF16),
            pltpu.VMEM((tile, _GL), F32), pltpu.VMEM((tile, _GL), F32),
            pltpu.VMEM((tile, n_main - _GL), BF16), pltpu.VMEM((tile, n_main - _GL), BF16),
            pltpu.VMEM((tile, HEAD_DIM), F32), pltpu.VMEM((tile, HEAD_DIM), F32),
            pltpu.VMEM((gate_rows, tile), F32), pltpu.VMEM((gate_rows, tile), F32),
            pltpu.VMEM((tile, N_BRANCH * WIDTH), BF16),
            pltpu.VMEM((n_seg, HEADS, HEAD_DIM, HEAD_DIM), F32),
            pltpu.VMEM((n_seg, HEADS, HEAD_DIM, HEAD_DIM), F32),
            pltpu.VMEM((n_seg, HEADS, HEAD_DIM, HEAD_DIM), F32),
            pltpu.VMEM((n_seg, HEADS, 1, HEAD_DIM), F32),
            pltpu.VMEM((n_seg, HEADS, 1, HEAD_DIM), F32),
        ],
        compiler_params=pltpu.CompilerParams(
            dimension_semantics=("arbitrary",), vmem_limit_bytes=VMEM_LIMIT_BYTES),
        name="mixer",
    )(xt, xt, s0, c0, n0, m0_rep, mk, mv,
      p["norm1"], p["w_main"], p["b_main"], p["w_gate"], p["w_gate_t"], p["b_gate_row"], p["b_gate_col"],
      p["lb_logits"], p["hg_norm"], p["ml_norm"], p["w_branch"], p["w_out"])
    return (outs[0].reshape(B, T, D),) + tuple(outs[1:])


_GELU_C = 0.7978845608028654


def _gelu_tanh_times(x, y):
    inner = x * (_GELU_C + (_GELU_C * 0.044715) * (x * x))
    return (0.5 + 0.5 * jnp.tanh(inner)) * (x * y)


def _ffn_kernel(h_ref, cv0_ref, norm2_ref, wup_ref, cw_ref, cb_ref, wdn_ref, fn_ref,
                y_ref, cv_out_ref, ug_ref, act_ref, *, n_seg, seg_len, d_ff):
    t = pl.program_id(1)
    last_t = pl.num_programs(1) - 1
    head = 8
    keep = CONV_W - 1
    d_model = h_ref.shape[-1]

    @pl.when(t == 0)
    def _first():
        for s in range(n_seg):
            ug_ref[s, 0:head - keep, :] = jnp.zeros((head - keep, d_ff), F32)
            ug_ref[s, head - keep:head, :] = cv0_ref[s]

    @pl.when(t > 0)
    def _carry():
        for s in range(n_seg):
            ug_ref[s, head - keep:head, :] = ug_ref[s, head + seg_len - keep:head + seg_len, :]

    h = h_ref[...].reshape(n_seg * seg_len, d_model)
    hb = _rms(h, norm2_ref[...]).astype(BF16)
    for c0 in range(0, d_ff, 256):
        cols = slice(c0, c0 + 256)
        up_g = _dot(hb, wup_ref[:, cols])
        up_v = _dot(hb, wup_ref[:, d_ff + c0:d_ff + c0 + 256])
        for s in range(n_seg):
            seg = slice(s * seg_len, (s + 1) * seg_len)
            ug_ref[s, head:head + seg_len, cols] = up_g[seg, :]
            rows = ug_ref[s, :, cols]
            conv = cb_ref[:, cols] + cw_ref[CONV_W - 1:CONV_W, cols] * rows[head:, :]
            for j in range(1, CONV_W):
                conv = conv + cw_ref[CONV_W - 1 - j:CONV_W - j, cols] * pltpu.roll(rows, j, axis=0)[head:, :]
            act_ref[seg, cols] = _gelu_tanh_times(conv, up_v[seg, :]).astype(BF16)
    h2 = h + _dot(act_ref[...], wdn_ref[...])
    y_ref[...] = _rms(h2, fn_ref[...]).reshape(n_seg, seg_len, d_model)

    @pl.when(t == last_t)
    def _store():
        for s in range(n_seg):
            cv_out_ref[s] = ug_ref[s, head + seg_len - keep:head + seg_len, :]


def _ffn(h, cv0, p):
    B, T, D = h.shape
    d_ff = p["w_down"].shape[0]
    if T % FFN_TOKEN_TILE == 0:
        n_seg, seg_len = 1, FFN_TOKEN_TILE
    else:
        n_seg, seg_len = max(1, min(B, TOKEN_TILE // T)), T
    assert B % n_seg == 0 and seg_len % 8 == 0
    keep = CONV_W - 1

    def full(shape):
        return pl.BlockSpec(shape, lambda b, t: (0,) * len(shape), pipeline_mode=pl.Buffered(1))

    kern = functools.partial(_ffn_kernel, n_seg=n_seg, seg_len=seg_len, d_ff=d_ff)
    return pl.pallas_call(
        kern,
        grid=(B // n_seg, T // seg_len),
        in_specs=[
            pl.BlockSpec((n_seg, seg_len, D), lambda b, t: (b, t, 0)),
            pl.BlockSpec((n_seg, keep, d_ff), lambda b, t: (b, 0, 0)),
            full((1, D)), full((D, 2 * d_ff)), full((CONV_W, d_ff)), full((1, d_ff)),
            full((d_ff, D)), full((1, D)),
        ],
        out_specs=[pl.BlockSpec((n_seg, seg_len, D), lambda b, t: (b, t, 0)),
                   pl.BlockSpec((n_seg, keep, d_ff), lambda b, t: (b, 0, 0))],
        out_shape=[jax.ShapeDtypeStruct((B, T, D), F32),
                   jax.ShapeDtypeStruct((B, keep, d_ff), F32)],
        scratch_shapes=[pltpu.VMEM((n_seg, 8 + seg_len, d_ff), F32),
                        pltpu.VMEM((n_seg * seg_len, d_ff), BF16)],
        compiler_params=pltpu.CompilerParams(
            dimension_semantics=("arbitrary", "arbitrary"), vmem_limit_bytes=VMEM_LIMIT_BYTES),
        name="conv_ffn",
    )(h, cv0, p["norm2"], p["w_up"], p["conv_w"], p["conv_b"], p["w_down"], p["final_norm"])


def _prepare(norm1, w_in, b_in, ml_fgate_bias, hg_lb_logits, hg_norm, ml_norm, w_branch, w_out,
             norm2, w_up, ffn_conv_w, ffn_conv_b, w_down, final_norm):
    D = w_in.shape[0]
    g0 = 8 * WIDTH
    g1 = g0 + 2 * HEADS
    w_in_b = w_in.astype(BF16)
    w_main = jnp.concatenate([w_in_b[:, :g0], w_in_b[:, g1:]], axis=1)
    b_main = jnp.concatenate([b_in[:g0], b_in[g1:]])[None, :]
    w_gate = jnp.zeros((D, HEAD_DIM), F32).at[:, :2 * HEADS].set(w_in[:, g0:g1]).astype(BF16)
    gate_rows = 16
    w_gate_t = jnp.zeros((gate_rows, D), F32).at[:2 * HEADS].set(w_in[:, g0:g1].T).astype(BF16)
    b_gate = b_in[g0:g1] + jnp.concatenate([jnp.zeros((HEADS,), F32), ml_fgate_bias])
    return dict(
        norm1=norm1[None, :], w_main=w_main, b_main=b_main, w_gate=w_gate, w_gate_t=w_gate_t,
        b_gate_row=jnp.zeros((1, HEAD_DIM), F32).at[0, :2 * HEADS].set(b_gate),
        b_gate_col=jnp.zeros((gate_rows, 1), F32).at[:2 * HEADS, 0].set(b_gate),
        lb_logits=hg_lb_logits, hg_norm=hg_norm[None, :], ml_norm=ml_norm[None, :],
        w_branch=w_branch.astype(BF16), w_out=w_out.astype(BF16),
        norm2=norm2[None, :], w_up=w_up.astype(BF16), conv_w=ffn_conv_w, conv_b=ffn_conv_b[None, :],
        w_down=w_down.astype(BF16), final_norm=final_norm[None, :])


def _layer(x, s_hg, s_c, s_n, s_m, s_cv, mk, mv, p):
    h1, hg, c, n, m = _mixer(x, s_hg, s_c, s_n, s_m, mk, mv, p)
    y, cv = _ffn(h1, s_cv, p)
    return y, hg, c, n, m[:, :, 0], cv


def kernel(x_prompt, x_sample, state_hgrn, state_mlstm_C, state_mlstm_n, state_mlstm_m, state_ffn_conv, cache_mem_k, cache_mem_v, mem_prompt, norm1, w_in, b_in, ml_fgate_bias, hg_lb_logits, hg_norm, ml_norm, mem_norm, w_mem_kv, w_branch, w_out, norm2, w_up, ffn_conv_w, ffn_conv_b, w_down, final_norm):
    depth = norm1.shape[0]
    assert depth == 1, "single-layer encoder"
    assert hg_lb_logits.shape[0] == depth + 1
    Bp = x_prompt.shape[0]
    Bs = x_sample.shape[0]
    M = mem_prompt.shape[1]
    d_ff = w_down.shape[1]
    l = 0
    p = _prepare(norm1[l], w_in[l], b_in[l], ml_fgate_bias[l], hg_lb_logits, hg_norm[l], ml_norm[l],
                 w_branch[l], w_out[l], norm2[l], w_up[l], ffn_conv_w[l], ffn_conv_b[l], w_down[l], final_norm)

    mk_p, mv_p, mk_pb, mv_pb = _memory_kv(mem_prompt, mem_norm[l], w_mem_kv[l])
    zeros = lambda *s: jnp.zeros(s, F32)
    yp, hg_p, c_p, n_p, m_p, cv_p = _layer(
        x_prompt, zeros(Bp, HEADS, HEAD_DIM, HEAD_DIM), zeros(Bp, HEADS, HEAD_DIM, HEAD_DIM),
        zeros(Bp, HEADS, HEAD_DIM), zeros(Bp, HEADS), zeros(Bp, CONV_W - 1, d_ff), mk_pb, mv_pb, p)
    ys, hg_s, c_s, n_s, m_s, cv_s = _layer(
        x_sample, state_hgrn[l], state_mlstm_C[l], state_mlstm_n[l], state_mlstm_m[l], state_ffn_conv[l],
        cache_mem_k[l].reshape(Bs, M, WIDTH).astype(BF16), cache_mem_v[l].reshape(Bs, M, WIDTH).astype(BF16), p)

    return (yp, ys,
            hg_p[None], c_p[None], n_p[None], m_p[None],
            mk_p[None], mv_p[None], cv_p[None],
            hg_s[None], c_s[None], n_s[None], m_s[None], cv_s[None])
```

```python
import functools

import jax
import jax.numpy as jnp
from jax import lax
from jax.experimental import pallas as pl
from jax.experimental.pallas import tpu as pltpu

HEADS = 4
HEAD_DIM = 128
WIDTH = HEADS * HEAD_DIM
N_BRANCH = 3
CONV_W = 3
NORM_EPS = 1e-6
RECURRENCE_CHUNK = 64
MLSTM_CHUNK = 128
TOKEN_TILE = 256
FFN_TOKEN_TILE = 1024
FFN_SUB_TILES = 2
COL_GROUP = 256
NORM_ROWS = 64
VMEM_LIMIT_BYTES = 60 * 1024 * 1024
HGRN2_LOG_DECAY_RANGE = 60.0

F32 = jnp.float32
BF16 = jnp.bfloat16

_HQ, _HF, _HI, _HG = 0, WIDTH, 2 * WIDTH, 3 * WIDTH
_MQ, _MK, _MV, _MO = 4 * WIDTH, 5 * WIDTH, 6 * WIDTH, 7 * WIDTH
_XQ = 8 * WIDTH
_GL = 9 * WIDTH

_HEAD_COLS = tuple(slice(h * HEAD_DIM, (h + 1) * HEAD_DIM) for h in range(HEADS))


def _dot(a, b):
    return jnp.dot(a, b, preferred_element_type=F32)


def _dot_nt(a, b):
    return lax.dot_general(a, b, (((1,), (1,)), ((), ())), preferred_element_type=F32)


def _dot_tn(a, b):
    return lax.dot_general(a, b, (((0,), (0,)), ((), ())), preferred_element_type=F32)


def _sigmoid(x):
    return 1.0 / (1.0 + jnp.exp(-x))


def _log_sigmoid(x):
    return jnp.minimum(x, 0.0) - jnp.log(1.0 + jnp.exp(-jnp.abs(x)))


def _rms(x, g):
    return x * lax.rsqrt(jnp.mean(x * x, axis=-1, keepdims=True) + NORM_EPS) * g


def _split_bf16(x):
    hi = x.astype(BF16)
    lo = (x - hi.astype(F32)).astype(BF16)
    return hi, lo


def _lower_tri(n):
    r = lax.broadcasted_iota(jnp.int32, (n, n), 0)
    c = lax.broadcasted_iota(jnp.int32, (n, n), 1)
    return r >= c


def _chunks(tile, length):
    return [slice(c * length, (c + 1) * length) for c in range(tile // length)]


def _memkv_kernel(mem_ref, g_ref, w_ref, k_ref, v_ref, kb_ref, vb_ref):
    u = _rms(mem_ref[0], g_ref[...]).astype(BF16)
    kv = _dot(u, w_ref[...])
    k = kv[:, :WIDTH]
    v = kv[:, WIDTH:]
    for h, hs in enumerate(_HEAD_COLS):
        k_ref[0, :, h, :] = k[:, hs]
        v_ref[0, :, h, :] = v[:, hs]
    kb_ref[0] = k.astype(BF16)
    vb_ref[0] = v.astype(BF16)


def _memory_kv(mem, g, w):
    B, M, D = mem.shape
    full = lambda shape: pl.BlockSpec(shape, lambda b: (0,) * len(shape))
    per_b = lambda shape: pl.BlockSpec((1,) + shape, lambda b: (b, 0, 0))
    return pl.pallas_call(
        _memkv_kernel,
        grid=(B,),
        in_specs=[per_b((M, D)), full((1, D)), full((D, 2 * WIDTH))],
        out_specs=[pl.BlockSpec((1, M, HEADS, HEAD_DIM), lambda b: (b, 0, 0, 0))] * 2 + [per_b((M, WIDTH))] * 2,
        out_shape=[jax.ShapeDtypeStruct((B, M, HEADS, HEAD_DIM), F32)] * 2
        + [jax.ShapeDtypeStruct((B, M, WIDTH), BF16)] * 2,
        compiler_params=pltpu.CompilerParams(dimension_semantics=("arbitrary",)),
        name="memory_kv",
    )(mem, g.reshape(1, D), w.astype(BF16))


def _normalise(x_ref, norm1_ref, ub_ref):
    n_rows = min(x_ref.shape[0], NORM_ROWS)
    for r0 in range(0, x_ref.shape[0], n_rows):
        rows = slice(r0, r0 + n_rows)
        ub_ref[rows, :] = _rms(x_ref[rows, :], norm1_ref[...]).astype(BF16)
        yield


def _project(ub_ref, wmain_ref, bmain_ref, wg_ref, wgt_ref, bgrow_ref, bgcol_ref,
             z_ref, sig_ref, gcol_ref, grow_ref):
    ub = ub_ref[...]
    n_main = wmain_ref.shape[1]
    gcol_ref[...] = _dot(ub, wg_ref[...]) + bgrow_ref[...]
    grow_ref[...] = _dot_nt(wgt_ref[...], ub) + bgcol_ref[...]
    for c0 in range(0, n_main, COL_GROUP):
        cols = slice(c0, c0 + COL_GROUP)
        zc = _dot(ub, wmain_ref[:, cols]) + bmain_ref[:, cols]
        if c0 < _GL:
            z_ref[:, cols] = zc
        else:
            sig_ref[:, c0 - _GL:c0 - _GL + COL_GROUP] = _sigmoid(zc).astype(BF16)
        yield


def _forget_floor(lbl_ref):
    logits = lbl_ref[...]
    e = jnp.exp(logits - jnp.max(logits, axis=0, keepdims=True))
    return e[0:1, :] / jnp.sum(e, axis=0, keepdims=True)


def _hgrn2(z_ref, lbl_ref, hgn_ref, br_ref, st_ref, st_save_ref, min_decay, *, tile, length, seg_len):
    L = length
    chunks = _chunks(tile, L)
    blocks_per_seg = seg_len // L
    lb = _forget_floor(lbl_ref)
    causal = _lower_tri(L)
    tri = jnp.where(causal, 1.0, 0.0).astype(BF16)
    for sg in range(tile // seg_len):
        for h in range(HEADS):
            st_save_ref[sg, h] = st_ref[sg, h]

    k, a_cum = [], []
    for rows in chunks:
        f = lb + (1.0 - lb) * _sigmoid(z_ref[rows, _HF:_HF + WIDTH])
        g_hi, g_lo = _split_bf16(jnp.log(f))
        a_cum.append(_dot(tri, g_hi) + _dot(tri, g_lo))
        k.append(1.0 - f)
        yield

    q_rel, k_rel, q_in, k_out, decay = [], [], [], [], []
    for c, rows in enumerate(chunks):
        a = a_cum[c]
        a_mid = a[L // 2 - 1:L // 2, :]
        a_end = a[L - 1:L, :]
        hq = z_ref[rows, _HQ:_HQ + WIDTH]
        qr = hq * _sigmoid(hq) * jnp.exp(a - a_mid)
        kr = k[c] * jnp.exp(a_mid - a)
        q_in.append((qr * jnp.exp(a_mid)).astype(BF16))
        k_out.append((kr * jnp.exp(a_end - a_mid)).astype(BF16))
        q_rel.append(qr.astype(BF16))
        k_rel.append(kr.astype(BF16))
        decay.append(jnp.exp(a_end))
        yield
    ends = a_cum[0][L - 1:L, :]
    for a in a_cum[1:]:
        ends = jnp.minimum(ends, a[L - 1:L, :])
    min_decay.append(jnp.min(ends))
    yield

    v = [z_ref[rows, _HI:_HI + WIDTH].astype(BF16) for rows in chunks]
    scores = [[jnp.where(causal, _dot_nt(q_rel[c][:, hs], k_rel[c][:, hs]), 0.0).astype(BF16)
               for hs in _HEAD_COLS] for c in range(len(chunks))]
    update = [[_dot_tn(v[c][:, hs], k_out[c][:, hs]) for hs in _HEAD_COLS] for c in range(len(chunks))]
    yield

    s_in = []
    for h, hs in enumerate(_HEAD_COLS):
        per_chunk = []
        for sg in range(tile // seg_len):
            s = st_ref[sg, h]
            for c in range(sg * blocks_per_seg, (sg + 1) * blocks_per_seg):
                per_chunk.append(s.astype(BF16))
                s = decay[c][:, hs] * s + update[c][h]
            st_ref[sg, h] = s
        s_in.append(per_chunk)
        yield

    hgn = hgn_ref[...]
    for c, rows in enumerate(chunks):
        gate = z_ref[rows, _HG:_HG + WIDTH]
        gate = gate * _sigmoid(gate)
        for h, hs in enumerate(_HEAD_COLS):
            o = _dot(scores[c][h], v[c][:, hs]) + _dot_nt(q_in[c][:, hs], s_in[h][c])
            br_ref[rows, hs] = (_rms(o, hgn[:, hs]) * gate[:, hs]).astype(BF16)
        yield


def _hgrn2_per_token(z_ref, lbl_ref, st_ref, st_save_ref, *, seg, seg_len):
    lb = _forget_floor(lbl_ref)
    st_ref = st_ref.at[seg]
    for h in range(HEADS):
        st_ref[h] = st_save_ref[seg, h]

    pad = 16
    first_row = lax.broadcasted_iota(jnp.int32, (pad, WIDTH), 0) == 0
    group = 8
    row_id = lax.broadcasted_iota(jnp.int32, (group, WIDTH), 0)

    def token_group(gi, carry):
        rows = pl.ds(pl.multiple_of(seg * seg_len + gi * group, group), group)
        hq = z_ref[rows, _HQ:_HQ + WIDTH]
        f = lb + (1.0 - lb) * _sigmoid(z_ref[rows, _HF:_HF + WIDTH])
        q = hq * _sigmoid(hq)
        k = 1.0 - f
        v = z_ref[rows, _HI:_HI + WIDTH]
        out = jnp.zeros((group, WIDTH), F32)
        for j in range(group):
            qj = jnp.where(first_row, q[j:j + 1, :], 0.0).astype(BF16)
            kj = jnp.where(first_row, k[j:j + 1, :], 0.0).astype(BF16)
            vj = jnp.where(first_row, v[j:j + 1, :], 0.0).astype(BF16)
            outs = []
            for h, hs in enumerate(_HEAD_COLS):
                s = f[j:j + 1, hs] * st_ref[h] + _dot_tn(vj[:, hs], kj[:, hs])
                st_ref[h] = s
                outs.append(_dot_nt(qj[:, hs], s.astype(BF16))[0:1, :])
            out = jnp.where(row_id == j, jnp.concatenate(outs, axis=-1), out)
        z_ref[rows, _HQ:_HQ + WIDTH] = out
        return carry

    lax.fori_loop(0, seg_len // group, token_group, 0)


def _redo_tile_per_token(x_ref, h_ref, z_ref, lbl_ref, hgn_ref, br_ref, sig_ref, wbr_ref, wout_ref,
                         st_ref, st_save_ref, *, tile, seg_len):
    for seg in range(tile // seg_len):
        _hgrn2_per_token(z_ref, lbl_ref, st_ref, st_save_ref, seg=seg, seg_len=seg_len)
    n_rows = min(tile, RECURRENCE_CHUNK)
    hgn = hgn_ref[...]

    def block(r, carry):
        rows = pl.ds(pl.multiple_of(r * n_rows, n_rows), n_rows)
        gate = z_ref[rows, _HG:_HG + WIDTH]
        gate = gate * _sigmoid(gate)
        for hs in _HEAD_COLS:
            o = z_ref[rows, _HQ + hs.start:_HQ + hs.stop]
            br_ref[rows, hs] = (_rms(o, hgn[:, hs]) * gate[:, hs]).astype(BF16)
        _drain(_merge(x_ref, h_ref, br_ref, sig_ref, wbr_ref, wout_ref, rows=rows))
        return carry

    lax.fori_loop(0, tile // n_rows, block, 0)


def _mlstm(z_ref, gcol_ref, grow_ref, mln_ref, br_ref, c_ref, n_ref, m_ref, *, tile, length, seg_len):
    L = length
    chunks = _chunks(tile, L)
    nc = len(chunks)
    seg_blocks = [range(sg * (seg_len // L), (sg + 1) * (seg_len // L)) for sg in range(tile // seg_len)]
    causal = _lower_tri(L)
    tri = jnp.where(causal, 1.0, 0.0).astype(BF16)
    tri_t = jnp.where(lax.broadcasted_iota(jnp.int32, (L, L), 0) <= lax.broadcasted_iota(jnp.int32, (L, L), 1),
                      1.0, 0.0).astype(BF16)
    lane = lax.broadcasted_iota(jnp.int32, (L, HEAD_DIM), 1)
    sub = lax.broadcasted_iota(jnp.int32, (grow_ref.shape[0], L), 0)

    g_c, g_r, b_c, b_r = [], [], [], []
    for rows in chunks:
        gc = gcol_ref[rows, :]
        gr = grow_ref[:, rows]
        fc_hi, fc_lo = _split_bf16(jnp.where(lane >= HEADS, _log_sigmoid(gc), 0.0))
        fr_hi, fr_lo = _split_bf16(jnp.where(sub >= HEADS, _log_sigmoid(gr), 0.0))
        g_c.append(gc)
        g_r.append(gr)
        b_c.append(_dot(tri, fc_hi) + _dot(tri, fc_lo))
        b_r.append(_dot(fr_hi, tri_t) + _dot(fr_lo, tri_t))
    yield

    blk = [[None] * HEADS for _ in range(nc)]
    for c, rows in enumerate(chunks):
        for h in range(HEADS):
            cols = slice(h * HEAD_DIM, (h + 1) * HEAD_DIM)
            i_col = g_c[c][:, h:h + 1]
            i_row = g_r[c][h:h + 1, :]
            b_col = b_c[c][:, HEADS + h:HEADS + h + 1]
            b_row = b_r[c][HEADS + h:HEADS + h + 1, :]
            b_last = b_row[:, L - 1:L]
            log_w = jnp.where(causal, b_col - b_row + i_row, -jnp.inf)
            q = z_ref[rows, _MQ + cols.start:_MQ + cols.stop]
            k = z_ref[rows, _MK + cols.start:_MK + cols.stop] * (HEAD_DIM ** -0.5)
            qb = q.astype(BF16)
            blk[c][h] = dict(
                b_col=b_col, b_last=b_last, log_w=log_w, q=q, k=k, qb=qb,
                w_max=jnp.max(log_w, axis=-1, keepdims=True),
                last_max=jnp.max(b_last - b_row + i_row, axis=-1, keepdims=True),
                log_last=b_last - b_col + i_col,
                qk=_dot_nt(qb, k.astype(BF16)),
                vb=z_ref[rows, _MV + cols.start:_MV + cols.stop].astype(BF16))
            if h % 2 == 1:
                yield

    for h in range(HEADS):
        for sg, blocks in enumerate(seg_blocks):
            m = m_ref[sg, h][:, 0:1]
            for c in blocks:
                d = blk[c][h]
                d["m_in"] = m
                m = jnp.maximum(d["b_last"] + m, d["last_max"])
                d["m_out"] = m
            m_ref[sg, h] = jnp.broadcast_to(m, (1, HEAD_DIM))
    yield

    for c in range(nc):
        for h in range(HEADS):
            d = blk[c][h]
            log_inter = d["b_col"] + d["m_in"]
            m_t = jnp.maximum(log_inter, d["w_max"])
            d["s"] = d["qk"] * jnp.exp(d["log_w"] - m_t)
            d["a"] = jnp.exp(log_inter - m_t)
            d["floor"] = jnp.exp(-m_t)
            kw = d["k"] * jnp.exp(d["log_last"] - d["m_out"])
            d["dec"] = jnp.exp(d["b_last"] + d["m_in"] - d["m_out"])
            d["c_upd"] = _dot_tn(kw.astype(BF16), d["vb"])
            d["n_upd"] = jnp.sum(kw, axis=0, keepdims=True)
            if h % 2 == 1:
                yield

    for h in range(HEADS):
        for sg, blocks in enumerate(seg_blocks):
            cm = c_ref[sg, h]
            n = n_ref[sg, h]
            for c in blocks:
                d = blk[c][h]
                d["c_in"] = cm.astype(BF16)
                d["n_in"] = n
                cm = d["dec"] * cm + d["c_upd"]
                n = d["dec"] * n + d["n_upd"]
            c_ref[sg, h] = cm
            n_ref[sg, h] = n
    yield

    mln = mln_ref[...]
    for c, rows in enumerate(chunks):
        for h, hs in enumerate(_HEAD_COLS):
            d = blk[c][h]
            num = _dot(d["s"].astype(BF16), d["vb"]) + d["a"] * _dot(d["qb"], d["c_in"])
            den = (jnp.sum(d["s"], axis=-1, keepdims=True)
                   + d["a"] * jnp.sum(d["q"] * d["n_in"], axis=-1, keepdims=True))
            hval = num / jnp.maximum(jnp.abs(den), d["floor"])
            og = z_ref[rows, _MO + hs.start:_MO + hs.stop]
            br_ref[rows, WIDTH + hs.start:WIDTH + hs.stop] = (
                _rms(hval, mln[:, hs]) * _sigmoid(og)).astype(BF16)
            if h % 2 == 1:
                yield


def _xattn(z_ref, mk_ref, mv_ref, br_ref, *, tile, seg_len):
    segs = _chunks(tile, seg_len)
    lg = [[_dot_nt(z_ref[rows, _XQ + hs.start:_XQ + hs.stop].astype(BF16), mk_ref[sg, :, hs]) * (HEAD_DIM ** -0.5)
           for sg, rows in enumerate(segs)] for hs in _HEAD_COLS]
    yield
    p = []
    for h in range(HEADS):
        per_seg = []
        for sg in range(len(segs)):
            e = jnp.exp(lg[h][sg] - jnp.max(lg[h][sg], axis=-1, keepdims=True))
            per_seg.append((e / jnp.sum(e, axis=-1, keepdims=True)).astype(BF16))
        p.append(per_seg)
        yield
    for h, hs in enumerate(_HEAD_COLS):
        for sg, rows in enumerate(segs):
            br_ref[rows, 2 * WIDTH + hs.start:2 * WIDTH + hs.stop] = _dot(
                p[h][sg], mv_ref[sg, :, hs]).astype(BF16)
    yield


def _merge(x_ref, h_ref, br_ref, sig_ref, wbr_ref, wout_ref, rows=slice(None)):
    d_model = x_ref.shape[-1]
    merged = None
    for i in range(N_BRANCH):
        y = (sig_ref[rows, i * d_model:(i + 1) * d_model]
             * _dot(br_ref[rows, i * WIDTH:(i + 1) * WIDTH], wbr_ref[i]))
        merged = y if merged is None else merged + y
        yield
    h_ref[rows, :] = x_ref[rows, :] + _dot(merged.astype(BF16), wout_ref[...])
    yield


_DONE = object()


def _rotate(*streams):
    lead = streams[0][0]
    while True:
        for gen, per_round in streams:
            for _ in range(per_round):
                if next(gen, _DONE) is _DONE and gen is lead:
                    return


def _drain(*gens):
    live = list(gens)
    while live:
        live = [gen for gen in live if next(gen, _DONE) is not _DONE]


def _mixer_kernel(xpair_ref, xnext_ref, s0_ref, c0_ref, n0_ref, m0_ref, mk_ref, mv_ref,
                  norm1_ref, wmain_ref, bmain_ref, wg_ref, wgt_ref, bgrow_ref, bgcol_ref,
                  lbl_ref, hgn_ref, mln_ref, wbr_ref, wout_ref,
                  h_ref, s_out_ref, c_out_ref, n_out_ref, m_out_ref,
                  uba_ref, ubb_ref, za_ref, zb_ref, siga_ref, sigb_ref, gca_ref, gcb_ref, gra_ref, grb_ref,
                  br_ref, st_ref, st_save_ref, c_ref, n_ref, m_ref,
                  *, tile, seg_len, hg_chunk, ml_chunk, tiles_per_row):
    g = pl.program_id(0)
    n_seg = tile // seg_len
    proj_w = (wmain_ref, bmain_ref, wg_ref, wgt_ref, bgrow_ref, bgcol_ref)
    ubs = (uba_ref, ubb_ref)
    bufs = ((za_ref, siga_ref, gca_ref, gra_ref), (zb_ref, sigb_ref, gcb_ref, grb_ref))

    @pl.when(g == 0)
    def _prologue():
        _drain(_normalise(xpair_ref.at[0], norm1_ref, ubs[0]))
        _drain(_project(ubs[0], *proj_w, *bufs[0]))
        _drain(_normalise(xpair_ref.at[1], norm1_ref, ubs[1]))

    for i in range(2):
        tile_in_row = (2 * g + i) % tiles_per_row if tiles_per_row > 1 else 0
        r0 = i * n_seg if tiles_per_row == 1 else 0

        def _load_state(r0=r0):
            for sg in range(n_seg):
                for h in range(HEADS):
                    st_ref[sg, h] = s0_ref[r0 + sg, h].T
                    c_ref[sg, h] = c0_ref[r0 + sg, h]
                    n_ref[sg, h] = n0_ref[r0 + sg, h:h + 1, :]
                    m_ref[sg, h] = m0_ref[r0 + sg, h:h + 1, :]

        def _store_state(r0=r0):
            for sg in range(n_seg):
                for h in range(HEADS):
                    s_out_ref[r0 + sg, h] = st_ref[sg, h].T
                    c_out_ref[r0 + sg, h] = c_ref[sg, h]
                    n_out_ref[r0 + sg, h:h + 1, :] = n_ref[sg, h]
                    m_out_ref[r0 + sg, h:h + 1, :] = m_ref[sg, h]

        if tiles_per_row == 1:
            _load_state()
        elif i == 0:
            pl.when(tile_in_row == 0)(_load_state)

        z_ref, sig_ref, gcol_ref, grow_ref = bufs[i]
        project_next = _project(ubs[1 - i], *proj_w, *bufs[1 - i])
        normalise_after = _normalise(xnext_ref.at[i], norm1_ref, ubs[i])
        min_decay = []
        hgrn2 = _hgrn2(z_ref, lbl_ref, hgn_ref, br_ref, st_ref, st_save_ref, min_decay,
                       tile=tile, length=hg_chunk, seg_len=seg_len)
        mlstm = _mlstm(z_ref, gcol_ref, grow_ref, mln_ref, br_ref, c_ref, n_ref, m_ref,
                       tile=tile, length=ml_chunk, seg_len=seg_len)
        xattn = _xattn(z_ref, mk_ref.at[r0:r0 + n_seg], mv_ref.at[r0:r0 + n_seg], br_ref,
                       tile=tile, seg_len=seg_len)
        _rotate((hgrn2, 1), (mlstm, 1), (xattn, 1), (project_next, 1))
        _drain(mlstm, xattn)
        _rotate((_merge(xpair_ref.at[i], h_ref.at[i], br_ref, sig_ref, wbr_ref, wout_ref), 1),
                (project_next, 3), (normalise_after, 1))
        _drain(project_next, normalise_after)

        @pl.when(min_decay[0] < -HGRN2_LOG_DECAY_RANGE)
        def _redo_hgrn2(i=i, z_ref=z_ref):
            _redo_tile_per_token(xpair_ref.at[i], h_ref.at[i], z_ref, lbl_ref, hgn_ref, br_ref, sig_ref,
                                 wbr_ref, wout_ref, st_ref, st_save_ref, tile=tile, seg_len=seg_len)

        if tiles_per_row == 1:
            _store_state()
        elif i == 1:
            pl.when(tile_in_row == tiles_per_row - 1)(_store_state)


def _mixer(x, s0, c0, n0, m0, mk, mv, p):
    B, T, D = x.shape
    if T % TOKEN_TILE == 0:
        tile = seg_len = TOKEN_TILE
    else:
        seg_len = T
        n_seg = max(n for n in range(1, B // 2 + 1) if B % (2 * n) == 0 and n * T <= TOKEN_TILE)
        tile = n_seg * T
    hg_chunk = RECURRENCE_CHUNK if seg_len % RECURRENCE_CHUNK == 0 else seg_len
    ml_chunk = MLSTM_CHUNK if seg_len % MLSTM_CHUNK == 0 else seg_len
    tiles_per_row = T // seg_len
    n_tiles = B * T // tile
    assert n_tiles % 2 == 0 and (tiles_per_row == 1 or tiles_per_row % 2 == 0)
    rows = 2 * (tile // seg_len) if tiles_per_row == 1 else 1
    steps_per_row_block = 1 if tiles_per_row == 1 else tiles_per_row // 2
    n_main = p["w_main"].shape[1]
    n_slots = p["lb_logits"].shape[0]
    M = mk.shape[1]
    gate_rows = p["w_gate_t"].shape[0]

    def full(shape):
        return pl.BlockSpec(shape, lambda g: (0,) * len(shape), pipeline_mode=pl.Buffered(1))

    def per_row(shape):
        return pl.BlockSpec((rows,) + shape, lambda g: (g // steps_per_row_block,) + (0,) * len(shape))

    state = per_row((HEADS, HEAD_DIM, HEAD_DIM))
    vec = per_row((HEADS, HEAD_DIM))
    m0_rep = jnp.broadcast_to(m0[:, :, None], (B, HEADS, HEAD_DIM))
    xt = x.reshape(n_tiles, tile, D)
    kern = functools.partial(_mixer_kernel, tile=tile, seg_len=seg_len, hg_chunk=hg_chunk, ml_chunk=ml_chunk,
                             tiles_per_row=tiles_per_row)
    n_seg = tile // seg_len
    outs = pl.pallas_call(
        kern,
        grid=(n_tiles // 2,),
        in_specs=[
            pl.BlockSpec((2, tile, D), lambda g: (g, 0, 0)),
            pl.BlockSpec((2, tile, D), lambda g: (jnp.minimum(g + 1, n_tiles // 2 - 1), 0, 0)),
            state, state, vec, vec, per_row((M, WIDTH)), per_row((M, WIDTH)),
            full((1, D)), full((D, n_main)), full((1, n_main)),
            full((D, HEAD_DIM)), full((gate_rows, D)), full((1, HEAD_DIM)), full((gate_rows, 1)),
            full((n_slots, WIDTH)), full((1, WIDTH)), full((1, WIDTH)),
            full((N_BRANCH, WIDTH, D)), full((D, D)),
        ],
        out_specs=[pl.BlockSpec((2, tile, D), lambda g: (g, 0, 0)), state, state, vec, vec],
        out_shape=[
            jax.ShapeDtypeStruct((n_tiles, tile, D), F32),
            jax.ShapeDtypeStruct((B, HEADS, HEAD_DIM, HEAD_DIM), F32),
            jax.ShapeDtypeStruct((B, HEADS, HEAD_DIM, HEAD_DIM), F32),
            jax.ShapeDtypeStruct((B, HEADS, HEAD_DIM), F32),
            jax.ShapeDtypeStruct((B, HEADS, HEAD_DIM), F32),
        ],
        scratch_shapes=[
            pltpu.VMEM((tile, D), BF16), pltpu.VMEM((tile, D), BF16),
            pltpu.VMEM((tile, _GL), F32), pltpu.VMEM((tile, _GL), F32),
            pltpu.VMEM((tile, n_main - _GL), BF16), pltpu.VMEM((tile, n_main - _GL), BF16),
            pltpu.VMEM((tile, HEAD_DIM), F32), pltpu.VMEM((tile, HEAD_DIM), F32),
            pltpu.VMEM((gate_rows, tile), F32), pltpu.VMEM((gate_rows, tile), F32),
            pltpu.VMEM((tile, N_BRANCH * WIDTH), BF16),
            pltpu.VMEM((n_seg, HEADS, HEAD_DIM, HEAD_DIM), F32),
            pltpu.VMEM((n_seg, HEADS, HEAD_DIM, HEAD_DIM), F32),
            pltpu.VMEM((n_seg, HEADS, HEAD_DIM, HEAD_DIM), F32),
            pltpu.VMEM((n_seg, HEADS, 1, HEAD_DIM), F32),
            pltpu.VMEM((n_seg, HEADS, 1, HEAD_DIM), F32),
        ],
        compiler_params=pltpu.CompilerParams(
            dimension_semantics=("arbitrary",), vmem_limit_bytes=VMEM_LIMIT_BYTES),
        name="mixer",
    )(xt, xt, s0, c0, n0, m0_rep, mk, mv,
      p["norm1"], p["w_main"], p["b_main"], p["w_gate"], p["w_gate_t"], p["b_gate_row"], p["b_gate_col"],
      p["lb_logits"], p["hg_norm"], p["ml_norm"], p["w_branch"], p["w_out"])
    return (outs[0].reshape(B, T, D),) + tuple(outs[1:])


_GELU_C = 0.7978845608028654


def _gelu_tanh_times(x, y):
    inner = x * (_GELU_C + (_GELU_C * 0.044715) * (x * x))
    return (0.5 + 0.5 * jnp.tanh(inner)) * (x * y)


def _ffn_kernel(h_ref, cv0_ref, norm2_ref, wup_ref, cw_ref, cb_ref, wdn_ref, fn_ref,
                y_ref, cv_out_ref, ug_ref, act_ref, hb_ref, *, n_seg, seg_len, n_sub, tiles_per_row, d_ff):
    t = pl.program_id(0) % tiles_per_row if tiles_per_row > 1 else 0
    last_t = tiles_per_row - 1
    head = 8
    keep = CONV_W - 1
    sub = n_seg * seg_len // n_sub
    if n_seg == 1:
        pieces = [[(slice(j * sub, (j + 1) * sub), 0, j * sub)] for j in range(n_sub)]
    else:
        pieces = [[(slice(s * seg_len, (s + 1) * seg_len), s, 0) for s in range(n_seg)]]

    def normalise(j):
        for r0 in range(j * sub, (j + 1) * sub, min(sub, 128)):
            rows = slice(r0, r0 + min(sub, 128))
            hb_ref[rows, :] = _rms(h_ref[rows, :], norm2_ref[...]).astype(BF16)
            yield

    def main(j):
        tile_rows = slice(j * sub, (j + 1) * sub)
        hb = hb_ref[tile_rows, :]
        for c0 in range(0, d_ff, 256):
            cols = slice(c0, c0 + 256)
            up_g = _dot(hb, wup_ref[:, cols])
            up_v = _dot(hb, wup_ref[:, d_ff + c0:d_ff + c0 + 256])
            for rows, s, off in pieces[j]:
                local = slice(rows.start - tile_rows.start, rows.stop - tile_rows.start)
                n = rows.stop - rows.start
                ug_ref[s, off + head:off + head + n, cols] = up_g[local, :]
                win = ug_ref[s, off:off + head + n, cols]
                conv = cb_ref[:, cols] + cw_ref[CONV_W - 1:CONV_W, cols] * win[head:, :]
                for k in range(1, CONV_W):
                    conv = conv + cw_ref[CONV_W - 1 - k:CONV_W - k, cols] * pltpu.roll(win, k, axis=0)[head:, :]
                act_ref[rows, cols] = _gelu_tanh_times(conv, up_v[local, :]).astype(BF16)
            yield
        y_ref[tile_rows, :] = h_ref[tile_rows, :] + _dot(act_ref[tile_rows, :], wdn_ref[...])
        yield

    def finalise(j):
        for r0 in range(j * sub, (j + 1) * sub, min(sub, 128)):
            rows = slice(r0, r0 + min(sub, 128))
            y_ref[rows, :] = _rms(y_ref[rows, :], fn_ref[...])
            yield

    def _first():
        for s in range(n_seg):
            ug_ref[s, 0:head - keep, :] = jnp.zeros((head - keep, d_ff), F32)
            ug_ref[s, head - keep:head, :] = cv0_ref[s]

    def _carry():
        for s in range(n_seg):
            ug_ref[s, head - keep:head, :] = ug_ref[s, head + seg_len - keep:head + seg_len, :]

    def _store():
        for s in range(n_seg):
            cv_out_ref[s] = ug_ref[s, head + seg_len - keep:head + seg_len, :]

    if tiles_per_row == 1:
        _first()
    else:
        pl.when(t == 0)(_first)
        pl.when(t > 0)(_carry)

    _drain(normalise(0))
    for j in range(n_sub):
        side = ([normalise(j + 1)] if j + 1 < n_sub else []) + ([finalise(j - 1)] if j > 0 else [])
        _rotate((main(j), 1), *[(gen, 1) for gen in side])
        _drain(*side)
    _drain(finalise(n_sub - 1))

    if tiles_per_row == 1:
        _store()
    else:
        pl.when(t == last_t)(_store)


def _ffn(h, cv0, p):
    B, T, D = h.shape
    d_ff = p["w_down"].shape[0]
    if T % FFN_TOKEN_TILE == 0:
        n_seg, seg_len = 1, FFN_TOKEN_TILE
    else:
        n_seg, seg_len = max(1, min(B, TOKEN_TILE // T)), T
    assert B % n_seg == 0 and seg_len % 8 == 0
    keep = CONV_W - 1
    rows = n_seg * seg_len
    tiles_per_row = T // seg_len
    n_sub = FFN_SUB_TILES if n_seg == 1 and seg_len % (FFN_SUB_TILES * 128) == 0 else 1

    def full(shape):
        return pl.BlockSpec(shape, lambda s: (0,) * len(shape), pipeline_mode=pl.Buffered(1))

    kern = functools.partial(_ffn_kernel, n_seg=n_seg, seg_len=seg_len, n_sub=n_sub,
                             tiles_per_row=tiles_per_row, d_ff=d_ff)
    y, cv = pl.pallas_call(
        kern,
        grid=(B * T // rows,),
        in_specs=[
            pl.BlockSpec((rows, D), lambda s: (s, 0)),
            pl.BlockSpec((n_seg, keep, d_ff), lambda s: (s // tiles_per_row, 0, 0)),
            full((1, D)), full((D, 2 * d_ff)), full((CONV_W, d_ff)), full((1, d_ff)),
            full((d_ff, D)), full((1, D)),
        ],
        out_specs=[pl.BlockSpec((rows, D), lambda s: (s, 0)),
                   pl.BlockSpec((n_seg, keep, d_ff), lambda s: (s // tiles_per_row, 0, 0))],
        out_shape=[jax.ShapeDtypeStruct((B * T, D), F32),
                   jax.ShapeDtypeStruct((B, keep, d_ff), F32)],
        scratch_shapes=[pltpu.VMEM((n_seg, 8 + seg_len, d_ff), F32),
                        pltpu.VMEM((rows, d_ff), BF16),
                        pltpu.VMEM((rows, D), BF16)],
        compiler_params=pltpu.CompilerParams(
            dimension_semantics=("arbitrary",), vmem_limit_bytes=VMEM_LIMIT_BYTES),
        name="conv_ffn",
    )(h.reshape(B * T, D), cv0, p["norm2"], p["w_up"], p["conv_w"], p["conv_b"], p["w_down"], p["final_norm"])
    return y.reshape(B, T, D), cv


def _regroup_kernel(w_ref, main_ref, gate_ref):
    g0 = 8 * WIDTH
    g1 = g0 + 2 * HEADS
    main_ref[:, :g0] = w_ref[:, :g0].astype(BF16)
    main_ref[:, g0:] = w_ref[:, g1:].astype(BF16)
    lane = lax.broadcasted_iota(jnp.int32, gate_ref.shape, 1)
    gate_ref[...] = jnp.where(lane < 2 * HEADS, w_ref[:, g0:g0 + HEAD_DIM], 0.0).astype(BF16)


def _regroup_w_in(w_in):
    D, n_in = w_in.shape
    n_main = n_in - 2 * HEADS
    rows = 128
    return pl.pallas_call(
        _regroup_kernel,
        grid=(D // rows,),
        in_specs=[pl.BlockSpec((rows, n_in), lambda r: (r, 0))],
        out_specs=[pl.BlockSpec((rows, n_main), lambda r: (r, 0)),
                   pl.BlockSpec((rows, HEAD_DIM), lambda r: (r, 0))],
        out_shape=[jax.ShapeDtypeStruct((D, n_main), BF16), jax.ShapeDtypeStruct((D, HEAD_DIM), BF16)],
        compiler_params=pltpu.CompilerParams(dimension_semantics=("arbitrary",)),
        name="regroup_w_in",
    )(w_in)


def _prepare(norm1, w_in, b_in, ml_fgate_bias, hg_lb_logits, hg_norm, ml_norm, w_branch, w_out,
             norm2, w_up, ffn_conv_w, ffn_conv_b, w_down, final_norm):
    D = w_in.shape[0]
    g0 = 8 * WIDTH
    g1 = g0 + 2 * HEADS
    w_main, w_gate = _regroup_w_in(w_in)
    b_main = jnp.concatenate([b_in[:g0], b_in[g1:]])[None, :]
    gate_rows = 16
    w_gate_t = w_gate[:, :gate_rows].T
    b_gate = b_in[g0:g1] + jnp.concatenate([jnp.zeros((HEADS,), F32), ml_fgate_bias])
    return dict(
        norm1=norm1[None, :], w_main=w_main, b_main=b_main, w_gate=w_gate, w_gate_t=w_gate_t,
        b_gate_row=jnp.zeros((1, HEAD_DIM), F32).at[0, :2 * HEADS].set(b_gate),
        b_gate_col=jnp.zeros((gate_rows, 1), F32).at[:2 * HEADS, 0].set(b_gate),
        lb_logits=hg_lb_logits, hg_norm=hg_norm[None, :], ml_norm=ml_norm[None, :],
        w_branch=w_branch.astype(BF16), w_out=w_out.astype(BF16),
        norm2=norm2[None, :], w_up=w_up.astype(BF16), conv_w=ffn_conv_w, conv_b=ffn_conv_b[None, :],
        w_down=w_down.astype(BF16), final_norm=final_norm[None, :])


def _layer(x, s_hg, s_c, s_n, s_m, s_cv, mk, mv, p):
    h1, hg, c, n, m = _mixer(x, s_hg, s_c, s_n, s_m, mk, mv, p)
    y, cv = _ffn(h1, s_cv, p)
    return y, hg, c, n, m[:, :, 0], cv


def kernel(x_prompt, x_sample, state_hgrn, state_mlstm_C, state_mlstm_n, state_mlstm_m, state_ffn_conv, cache_mem_k, cache_mem_v, mem_prompt, norm1, w_in, b_in, ml_fgate_bias, hg_lb_logits, hg_norm, ml_norm, mem_norm, w_mem_kv, w_branch, w_out, norm2, w_up, ffn_conv_w, ffn_conv_b, w_down, final_norm):
    depth = norm1.shape[0]
    assert depth == 1, "single-layer encoder"
    assert hg_lb_logits.shape[0] == depth + 1
    Bp = x_prompt.shape[0]
    Bs = x_sample.shape[0]
    M = mem_prompt.shape[1]
    d_ff = w_down.shape[1]
    l = 0
    p = _prepare(norm1[l], w_in[l], b_in[l], ml_fgate_bias[l], hg_lb_logits, hg_norm[l], ml_norm[l],
                 w_branch[l], w_out[l], norm2[l], w_up[l], ffn_conv_w[l], ffn_conv_b[l], w_down[l], final_norm)

    mk_p, mv_p, mk_pb, mv_pb = _memory_kv(mem_prompt, mem_norm[l], w_mem_kv[l])
    zeros = lambda *s: jnp.zeros(s, F32)
    yp, hg_p, c_p, n_p, m_p, cv_p = _layer(
        x_prompt, zeros(Bp, HEADS, HEAD_DIM, HEAD_DIM), zeros(Bp, HEADS, HEAD_DIM, HEAD_DIM),
        zeros(Bp, HEADS, HEAD_DIM), zeros(Bp, HEADS), zeros(Bp, CONV_W - 1, d_ff), mk_pb, mv_pb, p)
    ys, hg_s, c_s, n_s, m_s, cv_s = _layer(
        x_sample, state_hgrn[l], state_mlstm_C[l], state_mlstm_n[l], state_mlstm_m[l], state_ffn_conv[l],
        cache_mem_k[l].reshape(Bs, M, WIDTH).astype(BF16), cache_mem_v[l].reshape(Bs, M, WIDTH).astype(BF16), p)

    return (yp, ys,
            hg_p[None], c_p[None], n_p[None], m_p[None],
            mk_p[None], mv_p[None], cv_p[None],
            hg_s[None], c_s[None], n_s[None], m_s[None], cv_s[None])
```

```python
import functools

import jax
import jax.numpy as jnp
from jax import lax
from jax.experimental import pallas as pl
from jax.experimental.pallas import tpu as pltpu

HEADS = 4
HEAD_DIM = 128
WIDTH = HEADS * HEAD_DIM
N_BRANCH = 3
CONV_W = 3
NORM_EPS = 1e-6
RECURRENCE_CHUNK = 64
MLSTM_CHUNK = 128
TOKEN_TILE = 256
FFN_TOKEN_TILE = 1024
FFN_SUB_TILES = 2
COL_GROUP = 256
NORM_ROWS = 64
VMEM_LIMIT_BYTES = 60 * 1024 * 1024
HGRN2_LOG_DECAY_RANGE = 60.0

F32 = jnp.float32
BF16 = jnp.bfloat16

_HQ, _HF, _HI, _HG = 0, WIDTH, 2 * WIDTH, 3 * WIDTH
_MQ, _MK, _MV, _MO = 4 * WIDTH, 5 * WIDTH, 6 * WIDTH, 7 * WIDTH
_XQ = 8 * WIDTH
_GL = 9 * WIDTH

_HEAD_COLS = tuple(slice(h * HEAD_DIM, (h + 1) * HEAD_DIM) for h in range(HEADS))


def _dot(a, b):
    return jnp.dot(a, b, preferred_element_type=F32)


def _dot_nt(a, b):
    return lax.dot_general(a, b, (((1,), (1,)), ((), ())), preferred_element_type=F32)


def _dot_tn(a, b):
    return lax.dot_general(a, b, (((0,), (0,)), ((), ())), preferred_element_type=F32)


def _sigmoid(x):
    return 1.0 / (1.0 + jnp.exp(-x))


def _log_sigmoid(x):
    return jnp.minimum(x, 0.0) - jnp.log(1.0 + jnp.exp(-jnp.abs(x)))


def _rms(x, g):
    return x * lax.rsqrt(jnp.mean(x * x, axis=-1, keepdims=True) + NORM_EPS) * g


def _split_bf16(x):
    hi = x.astype(BF16)
    lo = (x - hi.astype(F32)).astype(BF16)
    return hi, lo


def _lower_tri(n):
    r = lax.broadcasted_iota(jnp.int32, (n, n), 0)
    c = lax.broadcasted_iota(jnp.int32, (n, n), 1)
    return r >= c


def _chunks(tile, length):
    return [slice(c * length, (c + 1) * length) for c in range(tile // length)]


def _memkv_kernel(mem_ref, g_ref, w_ref, k_ref, v_ref, kb_ref, vb_ref):
    u = _rms(mem_ref[0], g_ref[...]).astype(BF16)
    kv = _dot(u, w_ref[...])
    k = kv[:, :WIDTH]
    v = kv[:, WIDTH:]
    for h, hs in enumerate(_HEAD_COLS):
        k_ref[0, :, h, :] = k[:, hs]
        v_ref[0, :, h, :] = v[:, hs]
    kb_ref[0] = k.astype(BF16)
    vb_ref[0] = v.astype(BF16)


def _memory_kv(mem, g, w):
    B, M, D = mem.shape
    full = lambda shape: pl.BlockSpec(shape, lambda b: (0,) * len(shape))
    per_b = lambda shape: pl.BlockSpec((1,) + shape, lambda b: (b, 0, 0))
    return pl.pallas_call(
        _memkv_kernel,
        grid=(B,),
        in_specs=[per_b((M, D)), full((1, D)), full((D, 2 * WIDTH))],
        out_specs=[pl.BlockSpec((1, M, HEADS, HEAD_DIM), lambda b: (b, 0, 0, 0))] * 2 + [per_b((M, WIDTH))] * 2,
        out_shape=[jax.ShapeDtypeStruct((B, M, HEADS, HEAD_DIM), F32)] * 2
        + [jax.ShapeDtypeStruct((B, M, WIDTH), BF16)] * 2,
        compiler_params=pltpu.CompilerParams(dimension_semantics=("arbitrary",)),
        name="memory_kv",
    )(mem, g.reshape(1, D), w.astype(BF16))


def _normalise(x_ref, norm1_ref, ub_ref):
    n_rows = min(x_ref.shape[0], NORM_ROWS)
    for r0 in range(0, x_ref.shape[0], n_rows):
        rows = slice(r0, r0 + n_rows)
        ub_ref[rows, :] = _rms(x_ref[rows, :], norm1_ref[...]).astype(BF16)
        yield


def _project(ub_ref, wmain_ref, bmain_ref, wg_ref, wgt_ref, bgrow_ref, bgcol_ref,
             z_ref, sig_ref, gcol_ref, grow_ref):
    ub = ub_ref[...]
    n_main = wmain_ref.shape[1]
    gcol_ref[...] = _dot(ub, wg_ref[...]) + bgrow_ref[...]
    grow_ref[...] = _dot_nt(wgt_ref[...], ub) + bgcol_ref[...]
    for c0 in range(0, n_main, COL_GROUP):
        cols = slice(c0, c0 + COL_GROUP)
        zc = _dot(ub, wmain_ref[:, cols]) + bmain_ref[:, cols]
        if c0 < _GL:
            z_ref[:, cols] = zc
        else:
            sig_ref[:, c0 - _GL:c0 - _GL + COL_GROUP] = _sigmoid(zc).astype(BF16)
        yield


def _forget_floor(lbl_ref):
    logits = lbl_ref[...]
    e = jnp.exp(logits - jnp.max(logits, axis=0, keepdims=True))
    return e[0:1, :] / jnp.sum(e, axis=0, keepdims=True)


def _hgrn2(z_ref, lbl_ref, hgn_ref, br_ref, st_ref, st_save_ref, min_decay, *, tile, length, seg_len):
    L = length
    chunks = _chunks(tile, L)
    blocks_per_seg = seg_len // L
    lb = _forget_floor(lbl_ref)
    causal = _lower_tri(L)
    tri = jnp.where(causal, 1.0, 0.0).astype(BF16)
    for sg in range(tile // seg_len):
        for h in range(HEADS):
            st_save_ref[sg, h] = st_ref[sg, h]

    k, a_cum = [], []
    for rows in chunks:
        f = lb + (1.0 - lb) * _sigmoid(z_ref[rows, _HF:_HF + WIDTH])
        g_hi, g_lo = _split_bf16(jnp.log(f))
        a_cum.append(_dot(tri, g_hi) + _dot(tri, g_lo))
        k.append(1.0 - f)
        yield

    q_rel, k_rel, q_in, k_out, decay = [], [], [], [], []
    for c, rows in enumerate(chunks):
        a = a_cum[c]
        a_mid = a[L // 2 - 1:L // 2, :]
        a_end = a[L - 1:L, :]
        hq = z_ref[rows, _HQ:_HQ + WIDTH]
        qr = hq * _sigmoid(hq) * jnp.exp(a - a_mid)
        kr = k[c] * jnp.exp(a_mid - a)
        q_in.append((qr * jnp.exp(a_mid)).astype(BF16))
        k_out.append((kr * jnp.exp(a_end - a_mid)).astype(BF16))
        q_rel.append(qr.astype(BF16))
        k_rel.append(kr.astype(BF16))
        decay.append(jnp.exp(a_end))
        yield
    ends = a_cum[0][L - 1:L, :]
    for a in a_cum[1:]:
        ends = jnp.minimum(ends, a[L - 1:L, :])
    min_decay.append(jnp.min(ends))
    yield

    v = [z_ref[rows, _HI:_HI + WIDTH].astype(BF16) for rows in chunks]
    scores = [[jnp.where(causal, _dot_nt(q_rel[c][:, hs], k_rel[c][:, hs]), 0.0).astype(BF16)
               for hs in _HEAD_COLS] for c in range(len(chunks))]
    update = [[_dot_tn(v[c][:, hs], k_out[c][:, hs]) for hs in _HEAD_COLS] for c in range(len(chunks))]
    yield

    s_in = []
    for h, hs in enumerate(_HEAD_COLS):
        per_chunk = []
        for sg in range(tile // seg_len):
            s = st_ref[sg, h]
            for c in range(sg * blocks_per_seg, (sg + 1) * blocks_per_seg):
                per_chunk.append(s.astype(BF16))
                s = decay[c][:, hs] * s + update[c][h]
            st_ref[sg, h] = s
        s_in.append(per_chunk)
        yield

    hgn = hgn_ref[...]
    for c, rows in enumerate(chunks):
        gate = z_ref[rows, _HG:_HG + WIDTH]
        gate = gate * _sigmoid(gate)
        for h, hs in enumerate(_HEAD_COLS):
            o = _dot(scores[c][h], v[c][:, hs]) + _dot_nt(q_in[c][:, hs], s_in[h][c])
            br_ref[rows, hs] = (_rms(o, hgn[:, hs]) * gate[:, hs]).astype(BF16)
        yield


def _hgrn2_per_token(z_ref, lbl_ref, st_ref, st_save_ref, *, seg, seg_len):
    lb = _forget_floor(lbl_ref)
    st_ref = st_ref.at[seg]
    for h in range(HEADS):
        st_ref[h] = st_save_ref[seg, h]

    pad = 16
    first_row = lax.broadcasted_iota(jnp.int32, (pad, WIDTH), 0) == 0
    group = 8
    row_id = lax.broadcasted_iota(jnp.int32, (group, WIDTH), 0)

    def token_group(gi, carry):
        rows = pl.ds(pl.multiple_of(seg * seg_len + gi * group, group), group)
        hq = z_ref[rows, _HQ:_HQ + WIDTH]
        f = lb + (1.0 - lb) * _sigmoid(z_ref[rows, _HF:_HF + WIDTH])
        q = hq * _sigmoid(hq)
        k = 1.0 - f
        v = z_ref[rows, _HI:_HI + WIDTH]
        out = jnp.zeros((group, WIDTH), F32)
        for j in range(group):
            qj = jnp.where(first_row, q[j:j + 1, :], 0.0).astype(BF16)
            kj = jnp.where(first_row, k[j:j + 1, :], 0.0).astype(BF16)
            vj = jnp.where(first_row, v[j:j + 1, :], 0.0).astype(BF16)
            outs = []
            for h, hs in enumerate(_HEAD_COLS):
                s = f[j:j + 1, hs] * st_ref[h] + _dot_tn(vj[:, hs], kj[:, hs])
                st_ref[h] = s
                outs.append(_dot_nt(qj[:, hs], s.astype(BF16))[0:1, :])
            out = jnp.where(row_id == j, jnp.concatenate(outs, axis=-1), out)
        z_ref[rows, _HQ:_HQ + WIDTH] = out
        return carry

    lax.fori_loop(0, seg_len // group, token_group, 0)


def _redo_tile_per_token(x_ref, h_ref, z_ref, lbl_ref, hgn_ref, br_ref, sig_ref, wbr_ref, wout_ref,
                         st_ref, st_save_ref, *, tile, seg_len):
    for seg in range(tile // seg_len):
        _hgrn2_per_token(z_ref, lbl_ref, st_ref, st_save_ref, seg=seg, seg_len=seg_len)
    n_rows = min(tile, RECURRENCE_CHUNK)
    hgn = hgn_ref[...]

    def block(r, carry):
        rows = pl.ds(pl.multiple_of(r * n_rows, n_rows), n_rows)
        gate = z_ref[rows, _HG:_HG + WIDTH]
        gate = gate * _sigmoid(gate)
        for hs in _HEAD_COLS:
            o = z_ref[rows, _HQ + hs.start:_HQ + hs.stop]
            br_ref[rows, hs] = (_rms(o, hgn[:, hs]) * gate[:, hs]).astype(BF16)
        _drain(_merge(x_ref, h_ref, br_ref, sig_ref, wbr_ref, wout_ref, rows=rows))
        return carry

    lax.fori_loop(0, tile // n_rows, block, 0)


def _mlstm(z_ref, gcol_ref, grow_ref, mln_ref, br_ref, c_ref, n_ref, m_ref, *, tile, length, seg_len):
    L = length
    chunks = _chunks(tile, L)
    nc = len(chunks)
    seg_blocks = [range(sg * (seg_len // L), (sg + 1) * (seg_len // L)) for sg in range(tile // seg_len)]
    causal = _lower_tri(L)
    tri = jnp.where(causal, 1.0, 0.0).astype(BF16)
    tri_t = jnp.where(lax.broadcasted_iota(jnp.int32, (L, L), 0) <= lax.broadcasted_iota(jnp.int32, (L, L), 1),
                      1.0, 0.0).astype(BF16)
    lane = lax.broadcasted_iota(jnp.int32, (L, HEAD_DIM), 1)
    sub = lax.broadcasted_iota(jnp.int32, (grow_ref.shape[0], L), 0)

    g_c, g_r, b_c, b_r = [], [], [], []
    for rows in chunks:
        gc = gcol_ref[rows, :]
        gr = grow_ref[:, rows]
        fc_hi, fc_lo = _split_bf16(jnp.where(lane >= HEADS, _log_sigmoid(gc), 0.0))
        fr_hi, fr_lo = _split_bf16(jnp.where(sub >= HEADS, _log_sigmoid(gr), 0.0))
        g_c.append(gc)
        g_r.append(gr)
        b_c.append(_dot(tri, fc_hi) + _dot(tri, fc_lo))
        b_r.append(_dot(fr_hi, tri_t) + _dot(fr_lo, tri_t))
    yield

    blk = [[None] * HEADS for _ in range(nc)]
    for c, rows in enumerate(chunks):
        for h in range(HEADS):
            cols = slice(h * HEAD_DIM, (h + 1) * HEAD_DIM)
            i_col = g_c[c][:, h:h + 1]
            i_row = g_r[c][h:h + 1, :]
            b_col = b_c[c][:, HEADS + h:HEADS + h + 1]
            b_row = b_r[c][HEADS + h:HEADS + h + 1, :]
            b_last = b_row[:, L - 1:L]
            log_w = jnp.where(causal, b_col - b_row + i_row, -jnp.inf)
            q = z_ref[rows, _MQ + cols.start:_MQ + cols.stop]
            k = z_ref[rows, _MK + cols.start:_MK + cols.stop] * (HEAD_DIM ** -0.5)
            qb = q.astype(BF16)
            blk[c][h] = dict(
                b_col=b_col, b_last=b_last, log_w=log_w, q=q, k=k, qb=qb,
                w_max=jnp.max(log_w, axis=-1, keepdims=True),
                last_max=jnp.max(b_last - b_row + i_row, axis=-1, keepdims=True),
                log_last=b_last - b_col + i_col,
                qk=_dot_nt(qb, k.astype(BF16)),
                vb=z_ref[rows, _MV + cols.start:_MV + cols.stop].astype(BF16))
            if h % 2 == 1:
                yield

    for h in range(HEADS):
        for sg, blocks in enumerate(seg_blocks):
            m = m_ref[sg, h][:, 0:1]
            for c in blocks:
                d = blk[c][h]
                d["m_in"] = m
                m = jnp.maximum(d["b_last"] + m, d["last_max"])
                d["m_out"] = m
            m_ref[sg, h] = jnp.broadcast_to(m, (1, HEAD_DIM))
    yield

    for c in range(nc):
        for h in range(HEADS):
            d = blk[c][h]
            log_inter = d["b_col"] + d["m_in"]
            m_t = jnp.maximum(log_inter, d["w_max"])
            d["s"] = d["qk"] * jnp.exp(d["log_w"] - m_t)
            d["a"] = jnp.exp(log_inter - m_t)
            d["floor"] = jnp.exp(-m_t)
            kw = d["k"] * jnp.exp(d["log_last"] - d["m_out"])
            d["dec"] = jnp.exp(d["b_last"] + d["m_in"] - d["m_out"])
            d["c_upd"] = _dot_tn(kw.astype(BF16), d["vb"])
            d["n_upd"] = jnp.sum(kw, axis=0, keepdims=True)
            if h % 2 == 1:
                yield

    for h in range(HEADS):
        for sg, blocks in enumerate(seg_blocks):
            cm = c_ref[sg, h]
            n = n_ref[sg, h]
            for c in blocks:
                d = blk[c][h]
                d["c_in"] = cm.astype(BF16)
                d["n_in"] = n
                cm = d["dec"] * cm + d["c_upd"]
                n = d["dec"] * n + d["n_upd"]
            c_ref[sg, h] = cm
            n_ref[sg, h] = n
    yield

    mln = mln_ref[...]
    for c, rows in enumerate(chunks):
        for h, hs in enumerate(_HEAD_COLS):
            d = blk[c][h]
            num = _dot(d["s"].astype(BF16), d["vb"]) + d["a"] * _dot(d["qb"], d["c_in"])
            den = (jnp.sum(d["s"], axis=-1, keepdims=True)
                   + d["a"] * jnp.sum(d["q"] * d["n_in"], axis=-1, keepdims=True))
            hval = num / jnp.maximum(jnp.abs(den), d["floor"])
            og = z_ref[rows, _MO + hs.start:_MO + hs.stop]
            br_ref[rows, WIDTH + hs.start:WIDTH + hs.stop] = (
                _rms(hval, mln[:, hs]) * _sigmoid(og)).astype(BF16)
            if h % 2 == 1:
                yield


def _xattn(z_ref, mk_ref, mv_ref, br_ref, *, tile, seg_len):
    segs = _chunks(tile, seg_len)
    lg = [[_dot_nt(z_ref[rows, _XQ + hs.start:_XQ + hs.stop].astype(BF16), mk_ref[sg, :, hs]) * (HEAD_DIM ** -0.5)
           for sg, rows in enumerate(segs)] for hs in _HEAD_COLS]
    yield
    p = []
    for h in range(HEADS):
        per_seg = []
        for sg in range(len(segs)):
            e = jnp.exp(lg[h][sg] - jnp.max(lg[h][sg], axis=-1, keepdims=True))
            per_seg.append((e / jnp.sum(e, axis=-1, keepdims=True)).astype(BF16))
        p.append(per_seg)
        yield
    for h, hs in enumerate(_HEAD_COLS):
        for sg, rows in enumerate(segs):
            br_ref[rows, 2 * WIDTH + hs.start:2 * WIDTH + hs.stop] = _dot(
                p[h][sg], mv_ref[sg, :, hs]).astype(BF16)
    yield


def _merge(x_ref, h_ref, br_ref, sig_ref, wbr_ref, wout_ref, rows=slice(None)):
    d_model = x_ref.shape[-1]
    merged = None
    for i in range(N_BRANCH):
        y = (sig_ref[rows, i * d_model:(i + 1) * d_model]
             * _dot(br_ref[rows, i * WIDTH:(i + 1) * WIDTH], wbr_ref[i]))
        merged = y if merged is None else merged + y
        yield
    h_ref[rows, :] = x_ref[rows, :] + _dot(merged.astype(BF16), wout_ref[...])
    yield


_DONE = object()


def _rotate(*streams):
    lead = streams[0][0]
    while True:
        for gen, per_round in streams:
            for _ in range(per_round):
                if next(gen, _DONE) is _DONE and gen is lead:
                    return


def _drain(*gens):
    live = list(gens)
    while live:
        live = [gen for gen in live if next(gen, _DONE) is not _DONE]


def _mixer_kernel(xpair_ref, xnext_ref, s0_ref, c0_ref, n0_ref, m0_ref, mk_ref, mv_ref,
                  norm1_ref, wmain_ref, bmain_ref, wg_ref, wgt_ref, bgrow_ref, bgcol_ref,
                  lbl_ref, hgn_ref, mln_ref, wbr_ref, wout_ref,
                  h_ref, s_out_ref, c_out_ref, n_out_ref, m_out_ref,
                  uba_ref, ubb_ref, za_ref, zb_ref, siga_ref, sigb_ref, gca_ref, gcb_ref, gra_ref, grb_ref,
                  br_ref, st_ref, st_save_ref, c_ref, n_ref, m_ref,
                  *, tile, seg_len, hg_chunk, ml_chunk, tiles_per_row):
    g = pl.program_id(0)
    n_seg = tile // seg_len
    proj_w = (wmain_ref, bmain_ref, wg_ref, wgt_ref, bgrow_ref, bgcol_ref)
    ubs = (uba_ref, ubb_ref)
    bufs = ((za_ref, siga_ref, gca_ref, gra_ref), (zb_ref, sigb_ref, gcb_ref, grb_ref))

    @pl.when(g == 0)
    def _prologue():
        _drain(_normalise(xpair_ref.at[0], norm1_ref, ubs[0]))
        _drain(_project(ubs[0], *proj_w, *bufs[0]))
        _drain(_normalise(xpair_ref.at[1], norm1_ref, ubs[1]))

    for i in range(2):
        tile_in_row = (2 * g + i) % tiles_per_row if tiles_per_row > 1 else 0
        r0 = i * n_seg if tiles_per_row == 1 else 0

        def _load_state(r0=r0):
            for sg in range(n_seg):
                for h in range(HEADS):
                    st_ref[sg, h] = s0_ref[r0 + sg, h].T
                    c_ref[sg, h] = c0_ref[r0 + sg, h]
                    n_ref[sg, h] = n0_ref[r0 + sg, h:h + 1, :]
                    m_ref[sg, h] = m0_ref[r0 + sg, h:h + 1, :]

        def _store_state(r0=r0):
            for sg in range(n_seg):
                for h in range(HEADS):
                    s_out_ref[r0 + sg, h] = st_ref[sg, h].T
                    c_out_ref[r0 + sg, h] = c_ref[sg, h]
                    n_out_ref[r0 + sg, h:h + 1, :] = n_ref[sg, h]
                    m_out_ref[r0 + sg, h:h + 1, :] = m_ref[sg, h]

        if tiles_per_row == 1:
            _load_state()
        elif i == 0:
            pl.when(tile_in_row == 0)(_load_state)

        z_ref, sig_ref, gcol_ref, grow_ref = bufs[i]
        project_next = _project(ubs[1 - i], *proj_w, *bufs[1 - i])
        normalise_after = _normalise(xnext_ref.at[i], norm1_ref, ubs[i])
        min_decay = []
        hgrn2 = _hgrn2(z_ref, lbl_ref, hgn_ref, br_ref, st_ref, st_save_ref, min_decay,
                       tile=tile, length=hg_chunk, seg_len=seg_len)
        mlstm = _mlstm(z_ref, gcol_ref, grow_ref, mln_ref, br_ref, c_ref, n_ref, m_ref,
                       tile=tile, length=ml_chunk, seg_len=seg_len)
        xattn = _xattn(z_ref, mk_ref.at[r0:r0 + n_seg], mv_ref.at[r0:r0 + n_seg], br_ref,
                       tile=tile, seg_len=seg_len)
        _rotate((hgrn2, 1), (mlstm, 1), (xattn, 1), (project_next, 1))
        _drain(mlstm, xattn)
        _rotate((_merge(xpair_ref.at[i], h_ref.at[i], br_ref, sig_ref, wbr_ref, wout_ref), 1),
                (project_next, 3), (normalise_after, 1))
        _drain(project_next, normalise_after)

        @pl.when(min_decay[0] < -HGRN2_LOG_DECAY_RANGE)
        def _redo_hgrn2(i=i, z_ref=z_ref):
            _redo_tile_per_token(xpair_ref.at[i], h_ref.at[i], z_ref, lbl_ref, hgn_ref, br_ref, sig_ref,
                                 wbr_ref, wout_ref, st_ref, st_save_ref, tile=tile, seg_len=seg_len)

        if tiles_per_row == 1:
            _store_state()
        elif i == 1:
            pl.when(tile_in_row == tiles_per_row - 1)(_store_state)


def _mixer(x, s0, c0, n0, m0, mk, mv, p):
    B, T, D = x.shape
    if T % TOKEN_TILE == 0:
        tile = seg_len = TOKEN_TILE
    else:
        seg_len = T
        n_seg = max(n for n in range(1, B // 2 + 1) if B % (2 * n) == 0 and n * T <= TOKEN_TILE)
        tile = n_seg * T
    hg_chunk = RECURRENCE_CHUNK if seg_len % RECURRENCE_CHUNK == 0 else seg_len
    ml_chunk = MLSTM_CHUNK if seg_len % MLSTM_CHUNK == 0 else seg_len
    tiles_per_row = T // seg_len
    n_tiles = B * T // tile
    assert n_tiles % 2 == 0 and (tiles_per_row == 1 or tiles_per_row % 2 == 0)
    rows = 2 * (tile // seg_len) if tiles_per_row == 1 else 1
    steps_per_row_block = 1 if tiles_per_row == 1 else tiles_per_row // 2
    n_main = p["w_main"].shape[1]
    n_slots = p["lb_logits"].shape[0]
    M = mk.shape[1]
    gate_rows = p["w_gate_t"].shape[0]

    def full(shape):
        return pl.BlockSpec(shape, lambda g: (0,) * len(shape), pipeline_mode=pl.Buffered(1))

    def per_row(shape):
        return pl.BlockSpec((rows,) + shape, lambda g: (g // steps_per_row_block,) + (0,) * len(shape))

    state = per_row((HEADS, HEAD_DIM, HEAD_DIM))
    vec = per_row((HEADS, HEAD_DIM))
    m0_rep = jnp.broadcast_to(m0[:, :, None], (B, HEADS, HEAD_DIM))
    xt = x.reshape(n_tiles, tile, D)
    kern = functools.partial(_mixer_kernel, tile=tile, seg_len=seg_len, hg_chunk=hg_chunk, ml_chunk=ml_chunk,
                             tiles_per_row=tiles_per_row)
    n_seg = tile // seg_len
    outs = pl.pallas_call(
        kern,
        grid=(n_tiles // 2,),
        in_specs=[
            pl.BlockSpec((2, tile, D), lambda g: (g, 0, 0)),
            pl.BlockSpec((2, tile, D), lambda g: (jnp.minimum(g + 1, n_tiles // 2 - 1), 0, 0)),
            state, state, vec, vec, per_row((M, WIDTH)), per_row((M, WIDTH)),
            full((1, D)), full((D, n_main)), full((1, n_main)),
            full((D, HEAD_DIM)), full((gate_rows, D)), full((1, HEAD_DIM)), full((gate_rows, 1)),
            full((n_slots, WIDTH)), full((1, WIDTH)), full((1, WIDTH)),
            full((N_BRANCH, WIDTH, D)), full((D, D)),
        ],
        out_specs=[pl.BlockSpec((2, tile, D), lambda g: (g, 0, 0)), state, state, vec, vec],
        out_shape=[
            jax.ShapeDtypeStruct((n_tiles, tile, D), F32),
            jax.ShapeDtypeStruct((B, HEADS, HEAD_DIM, HEAD_DIM), F32),
            jax.ShapeDtypeStruct((B, HEADS, HEAD_DIM, HEAD_DIM), F32),
            jax.ShapeDtypeStruct((B, HEADS, HEAD_DIM), F32),
            jax.ShapeDtypeStruct((B, HEADS, HEAD_DIM), F32),
        ],
        scratch_shapes=[
            pltpu.VMEM((tile, D), BF16), pltpu.VMEM((tile, D), BF16),
            pltpu.VMEM((tile, _GL), F32), pltpu.VMEM((tile, _GL), F32),
            pltpu.VMEM((tile, n_main - _GL), BF16), pltpu.VMEM((tile, n_main - _GL), BF16),
            pltpu.VMEM((tile, HEAD_DIM), F32), pltpu.VMEM((tile, HEAD_DIM), F32),
            pltpu.VMEM((gate_rows, tile), F32), pltpu.VMEM((gate_rows, tile), F32),
            pltpu.VMEM((tile, N_BRANCH * WIDTH), BF16),
            pltpu.VMEM((n_seg, HEADS, HEAD_DIM, HEAD_DIM), F32),
            pltpu.VMEM((n_seg, HEADS, HEAD_DIM, HEAD_DIM), F32),
            pltpu.VMEM((n_seg, HEADS, HEAD_DIM, HEAD_DIM), F32),
            pltpu.VMEM((n_seg, HEADS, 1, HEAD_DIM), F32),
            pltpu.VMEM((n_seg, HEADS, 1, HEAD_DIM), F32),
        ],
        compiler_params=pltpu.CompilerParams(
            dimension_semantics=("arbitrary",), vmem_limit_bytes=VMEM_LIMIT_BYTES),
        name="mixer",
    )(xt, xt, s0, c0, n0, m0_rep, mk, mv,
      p["norm1"], p["w_main"], p["b_main"], p["w_gate"], p["w_gate_t"], p["b_gate_row"], p["b_gate_col"],
      p["lb_logits"], p["hg_norm"], p["ml_norm"], p["w_branch"], p["w_out"])
    return (outs[0].reshape(B, T, D),) + tuple(outs[1:])


_GELU_C = 0.7978845608028654


def _gelu_tanh_times(x, y):
    inner = x * (_GELU_C + (_GELU_C * 0.044715) * (x * x))
    return (0.5 + 0.5 * jnp.tanh(inner)) * (x * y)


def _ffn_kernel(h_ref, cv0_ref, norm2_ref, wup_ref, cw_ref, cb_ref, wdn_ref, fn_ref,
                y_ref, cv_out_ref, ug_ref, act_ref, hb_ref, *, n_seg, seg_len, n_sub, tiles_per_row, d_ff):
    t = pl.program_id(0) % tiles_per_row if tiles_per_row > 1 else 0
    last_t = tiles_per_row - 1
    head = 8
    keep = CONV_W - 1
    sub = n_seg * seg_len // n_sub
    if n_seg == 1:
        pieces = [[(slice(j * sub, (j + 1) * sub), 0, j * sub)] for j in range(n_sub)]
    else:
        pieces = [[(slice(s * seg_len, (s + 1) * seg_len), s, 0) for s in range(n_seg)]]

    def normalise(j):
        for r0 in range(j * sub, (j + 1) * sub, min(sub, 128)):
            rows = slice(r0, r0 + min(sub, 128))
            hb_ref[rows, :] = _rms(h_ref[rows, :], norm2_ref[...]).astype(BF16)
            yield

    def main(j):
        tile_rows = slice(j * sub, (j + 1) * sub)
        hb = hb_ref[tile_rows, :]
        for c0 in range(0, d_ff, 256):
            cols = slice(c0, c0 + 256)
            up_g = _dot(hb, wup_ref[:, cols])
            up_v = _dot(hb, wup_ref[:, d_ff + c0:d_ff + c0 + 256])
            for rows, s, off in pieces[j]:
                local = slice(rows.start - tile_rows.start, rows.stop - tile_rows.start)
                n = rows.stop - rows.start
                ug_ref[s, off + head:off + head + n, cols] = up_g[local, :]
                win = ug_ref[s, off:off + head + n, cols]
                conv = cb_ref[:, cols] + cw_ref[CONV_W - 1:CONV_W, cols] * win[head:, :]
                for k in range(1, CONV_W):
                    conv = conv + cw_ref[CONV_W - 1 - k:CONV_W - k, cols] * pltpu.roll(win, k, axis=0)[head:, :]
                act_ref[rows, cols] = _gelu_tanh_times(conv, up_v[local, :]).astype(BF16)
            yield
        y_ref[tile_rows, :] = h_ref[tile_rows, :] + _dot(act_ref[tile_rows, :], wdn_ref[...])
        yield

    def finalise(j):
        for r0 in range(j * sub, (j + 1) * sub, min(sub, 128)):
            rows = slice(r0, r0 + min(sub, 128))
            y_ref[rows, :] = _rms(y_ref[rows, :], fn_ref[...])
            yield

    def _first():
        for s in range(n_seg):
            ug_ref[s, 0:head - keep, :] = jnp.zeros((head - keep, d_ff), F32)
            ug_ref[s, head - keep:head, :] = cv0_ref[s]

    def _carry():
        for s in range(n_seg):
            ug_ref[s, head - keep:head, :] = ug_ref[s, head + seg_len - keep:head + seg_len, :]

    def _store():
        for s in range(n_seg):
            cv_out_ref[s] = ug_ref[s, head + seg_len - keep:head + seg_len, :]

    if tiles_per_row == 1:
        _first()
    else:
        pl.when(t == 0)(_first)
        pl.when(t > 0)(_carry)

    _drain(normalise(0))
    for j in range(n_sub):
        side = ([normalise(j + 1)] if j + 1 < n_sub else []) + ([finalise(j - 1)] if j > 0 else [])
        _rotate((main(j), 1), *[(gen, 1) for gen in side])
        _drain(*side)
    _drain(finalise(n_sub - 1))

    if tiles_per_row == 1:
        _store()
    else:
        pl.when(t == last_t)(_store)


def _ffn(h, cv0, p):
    B, T, D = h.shape
    d_ff = p["w_down"].shape[0]
    if T % FFN_TOKEN_TILE == 0:
        n_seg, seg_len = 1, FFN_TOKEN_TILE
    else:
        n_seg, seg_len = max(1, min(B, TOKEN_TILE // T)), T
    assert B % n_seg == 0 and seg_len % 8 == 0
    keep = CONV_W - 1
    rows = n_seg * seg_len
    tiles_per_row = T // seg_len
    n_sub = FFN_SUB_TILES if n_seg == 1 and seg_len % (FFN_SUB_TILES * 128) == 0 else 1

    def full(shape):
        return pl.BlockSpec(shape, lambda s: (0,) * len(shape), pipeline_mode=pl.Buffered(1))

    kern = functools.partial(_ffn_kernel, n_seg=n_seg, seg_len=seg_len, n_sub=n_sub,
                             tiles_per_row=tiles_per_row, d_ff=d_ff)
    y, cv = pl.pallas_call(
        kern,
        grid=(B * T // rows,),
        in_specs=[
            pl.BlockSpec((rows, D), lambda s: (s, 0)),
            pl.BlockSpec((n_seg, keep, d_ff), lambda s: (s // tiles_per_row, 0, 0)),
            full((1, D)), full((D, 2 * d_ff)), full((CONV_W, d_ff)), full((1, d_ff)),
            full((d_ff, D)), full((1, D)),
        ],
        out_specs=[pl.BlockSpec((rows, D), lambda s: (s, 0)),
                   pl.BlockSpec((n_seg, keep, d_ff), lambda s: (s // tiles_per_row, 0, 0))],
        out_shape=[jax.ShapeDtypeStruct((B * T, D), F32),
                   jax.ShapeDtypeStruct((B, keep, d_ff), F32)],
        scratch_shapes=[pltpu.VMEM((n_seg, 8 + seg_len, d_ff), F32),
                        pltpu.VMEM((rows, d_ff), BF16),
                        pltpu.VMEM((rows, D), BF16)],
        compiler_params=pltpu.CompilerParams(
            dimension_semantics=("arbitrary",), vmem_limit_bytes=VMEM_LIMIT_BYTES),
        name="conv_ffn",
    )(h.reshape(B * T, D), cv0, p["norm2"], p["w_up"], p["conv_w"], p["conv_b"], p["w_down"], p["final_norm"])
    return y.reshape(B, T, D), cv


def _regroup_kernel(w_ref, main_ref, gate_ref):
    g0 = 8 * WIDTH
    g1 = g0 + 2 * HEADS
    main_ref[:, :g0] = w_ref[:, :g0].astype(BF16)
    main_ref[:, g0:] = w_ref[:, g1:].astype(BF16)
    lane = lax.broadcasted_iota(jnp.int32, gate_ref.shape, 1)
    gate_ref[...] = jnp.where(lane < 2 * HEADS, w_ref[:, g0:g0 + HEAD_DIM], 0.0).astype(BF16)


def _regroup_w_in(w_in, layer):
    _, D, n_in = w_in.shape
    n_main = n_in - 2 * HEADS
    rows = 128
    return pl.pallas_call(
        _regroup_kernel,
        grid=(D // rows,),
        in_specs=[pl.BlockSpec((None, rows, n_in), lambda r: (layer, r, 0))],
        out_specs=[pl.BlockSpec((rows, n_main), lambda r: (r, 0)),
                   pl.BlockSpec((rows, HEAD_DIM), lambda r: (r, 0))],
        out_shape=[jax.ShapeDtypeStruct((D, n_main), BF16), jax.ShapeDtypeStruct((D, HEAD_DIM), BF16)],
        compiler_params=pltpu.CompilerParams(dimension_semantics=("arbitrary",)),
        name="regroup_w_in",
    )(w_in)


def _prepare(layer, norm1, w_in_all, b_in, ml_fgate_bias, hg_lb_logits, hg_norm, ml_norm, w_branch, w_out,
             norm2, w_up, ffn_conv_w, ffn_conv_b, w_down, final_norm):
    g0 = 8 * WIDTH
    g1 = g0 + 2 * HEADS
    w_main, w_gate = _regroup_w_in(w_in_all, layer)
    b_main = jnp.concatenate([b_in[:g0], b_in[g1:]])[None, :]
    gate_rows = 16
    w_gate_t = w_gate[:, :gate_rows].T
    b_gate = b_in[g0:g1] + jnp.concatenate([jnp.zeros((HEADS,), F32), ml_fgate_bias])
    return dict(
        norm1=norm1[None, :], w_main=w_main, b_main=b_main, w_gate=w_gate, w_gate_t=w_gate_t,
        b_gate_row=jnp.zeros((1, HEAD_DIM), F32).at[0, :2 * HEADS].set(b_gate),
        b_gate_col=jnp.zeros((gate_rows, 1), F32).at[:2 * HEADS, 0].set(b_gate),
        lb_logits=hg_lb_logits, hg_norm=hg_norm[None, :], ml_norm=ml_norm[None, :],
        w_branch=w_branch.astype(BF16), w_out=w_out.astype(BF16),
        norm2=norm2[None, :], w_up=w_up.astype(BF16), conv_w=ffn_conv_w, conv_b=ffn_conv_b[None, :],
        w_down=w_down.astype(BF16), final_norm=final_norm[None, :])


def _layer(x, s_hg, s_c, s_n, s_m, s_cv, mk, mv, p):
    h1, hg, c, n, m = _mixer(x, s_hg, s_c, s_n, s_m, mk, mv, p)
    y, cv = _ffn(h1, s_cv, p)
    return y, hg, c, n, m[:, :, 0], cv


def kernel(x_prompt, x_sample, state_hgrn, state_mlstm_C, state_mlstm_n, state_mlstm_m, state_ffn_conv, cache_mem_k, cache_mem_v, mem_prompt, norm1, w_in, b_in, ml_fgate_bias, hg_lb_logits, hg_norm, ml_norm, mem_norm, w_mem_kv, w_branch, w_out, norm2, w_up, ffn_conv_w, ffn_conv_b, w_down, final_norm):
    depth = norm1.shape[0]
    assert depth == 1, "single-layer encoder"
    assert hg_lb_logits.shape[0] == depth + 1
    Bp = x_prompt.shape[0]
    Bs = x_sample.shape[0]
    M = mem_prompt.shape[1]
    d_ff = w_down.shape[1]
    l = 0
    p = _prepare(l, norm1[l], w_in, b_in[l], ml_fgate_bias[l], hg_lb_logits, hg_norm[l], ml_norm[l],
                 w_branch[l], w_out[l], norm2[l], w_up[l], ffn_conv_w[l], ffn_conv_b[l], w_down[l], final_norm)

    mk_p, mv_p, mk_pb, mv_pb = _memory_kv(mem_prompt, mem_norm[l], w_mem_kv[l])
    zeros = lambda *s: jnp.zeros(s, F32)
    yp, hg_p, c_p, n_p, m_p, cv_p = _layer(
        x_prompt, zeros(Bp, HEADS, HEAD_DIM, HEAD_DIM), zeros(Bp, HEADS, HEAD_DIM, HEAD_DIM),
        zeros(Bp, HEADS, HEAD_DIM), zeros(Bp, HEADS), zeros(Bp, CONV_W - 1, d_ff), mk_pb, mv_pb, p)
    ys, hg_s, c_s, n_s, m_s, cv_s = _layer(
        x_sample, state_hgrn[l], state_mlstm_C[l], state_mlstm_n[l], state_mlstm_m[l], state_ffn_conv[l],
        cache_mem_k[l].reshape(Bs, M, WIDTH).astype(BF16), cache_mem_v[l].reshape(Bs, M, WIDTH).astype(BF16), p)

    return (yp, ys,
            hg_p[None], c_p[None], n_p[None], m_p[None],
            mk_p[None], mv_p[None], cv_p[None],
            hg_s[None], c_s[None], n_s[None], m_s[None], cv_s[None])
```

```python
import functools

import jax
import jax.numpy as jnp
from jax import lax
from jax.experimental import pallas as pl
from jax.experimental.pallas import tpu as pltpu

HEADS = 4
HEAD_DIM = 128
WIDTH = HEADS * HEAD_DIM
N_BRANCH = 3
CONV_W = 3
NORM_EPS = 1e-6
RECURRENCE_CHUNK = 64
MLSTM_CHUNK = 128
TOKEN_TILE = 256
FFN_TOKEN_TILE = 1024
FFN_SUB_TILES = 2
COL_GROUP = 256
NORM_ROWS = 64
VMEM_LIMIT_BYTES = 60 * 1024 * 1024
HGRN2_LOG_DECAY_RANGE = 60.0

F32 = jnp.float32
BF16 = jnp.bfloat16

_HQ, _HF, _HI, _HG = 0, WIDTH, 2 * WIDTH, 3 * WIDTH
_MQ, _MK, _MV, _MO = 4 * WIDTH, 5 * WIDTH, 6 * WIDTH, 7 * WIDTH
_XQ = 8 * WIDTH
_GL = 9 * WIDTH

_HEAD_COLS = tuple(slice(h * HEAD_DIM, (h + 1) * HEAD_DIM) for h in range(HEADS))


def _dot(a, b):
    return jnp.dot(a, b, preferred_element_type=F32)


def _dot_nt(a, b):
    return lax.dot_general(a, b, (((1,), (1,)), ((), ())), preferred_element_type=F32)


def _dot_tn(a, b):
    return lax.dot_general(a, b, (((0,), (0,)), ((), ())), preferred_element_type=F32)


def _sigmoid(x):
    return 1.0 / (1.0 + jnp.exp(-x))


def _log_sigmoid(x):
    return jnp.minimum(x, 0.0) - jnp.log(1.0 + jnp.exp(-jnp.abs(x)))


def _rms(x, g):
    return x * lax.rsqrt(jnp.mean(x * x, axis=-1, keepdims=True) + NORM_EPS) * g


def _split_bf16(x):
    hi = x.astype(BF16)
    lo = (x - hi.astype(F32)).astype(BF16)
    return hi, lo


def _lower_tri(n):
    r = lax.broadcasted_iota(jnp.int32, (n, n), 0)
    c = lax.broadcasted_iota(jnp.int32, (n, n), 1)
    return r >= c


def _chunks(tile, length):
    return [slice(c * length, (c + 1) * length) for c in range(tile // length)]


def _memkv_kernel(mem_ref, g_ref, w_ref, k_ref, v_ref, kb_ref, vb_ref):
    u = _rms(mem_ref[0], g_ref[...]).astype(BF16)
    kv = _dot(u, w_ref[...])
    k = kv[:, :WIDTH]
    v = kv[:, WIDTH:]
    for h, hs in enumerate(_HEAD_COLS):
        k_ref[0, :, h, :] = k[:, hs]
        v_ref[0, :, h, :] = v[:, hs]
    kb_ref[0] = k.astype(BF16)
    vb_ref[0] = v.astype(BF16)


def _memory_kv(mem, g, w):
    B, M, D = mem.shape
    full = lambda shape: pl.BlockSpec(shape, lambda b: (0,) * len(shape))
    per_b = lambda shape: pl.BlockSpec((1,) + shape, lambda b: (b, 0, 0))
    return pl.pallas_call(
        _memkv_kernel,
        grid=(B,),
        in_specs=[per_b((M, D)), full((1, D)), full((D, 2 * WIDTH))],
        out_specs=[pl.BlockSpec((1, M, HEADS, HEAD_DIM), lambda b: (b, 0, 0, 0))] * 2 + [per_b((M, WIDTH))] * 2,
        out_shape=[jax.ShapeDtypeStruct((B, M, HEADS, HEAD_DIM), F32)] * 2
        + [jax.ShapeDtypeStruct((B, M, WIDTH), BF16)] * 2,
        compiler_params=pltpu.CompilerParams(dimension_semantics=("arbitrary",)),
        name="memory_kv",
    )(mem, g.reshape(1, D), w.astype(BF16))


def _normalise(x_ref, norm1_ref, ub_ref):
    n_rows = min(x_ref.shape[0], NORM_ROWS)
    for r0 in range(0, x_ref.shape[0], n_rows):
        rows = slice(r0, r0 + n_rows)
        ub_ref[rows, :] = _rms(x_ref[rows, :], norm1_ref[...]).astype(BF16)
        yield


def _project(ub_ref, wmain_ref, bmain_ref, wg_ref, wgt_ref, bgrow_ref, bgcol_ref,
             z_ref, sig_ref, gcol_ref, grow_ref):
    ub = ub_ref[...]
    n_main = wmain_ref.shape[1]
    gcol_ref[...] = _dot(ub, wg_ref[...]) + bgrow_ref[...]
    grow_ref[...] = _dot_nt(wgt_ref[...], ub) + bgcol_ref[...]
    for c0 in range(0, n_main, COL_GROUP):
        cols = slice(c0, c0 + COL_GROUP)
        zc = _dot(ub, wmain_ref[:, cols]) + bmain_ref[:, cols]
        if c0 < _GL:
            z_ref[:, cols] = zc
        else:
            sig_ref[:, c0 - _GL:c0 - _GL + COL_GROUP] = _sigmoid(zc).astype(BF16)
        yield


def _forget_floor(lbl_ref):
    logits = lbl_ref[...]
    e = jnp.exp(logits - jnp.max(logits, axis=0, keepdims=True))
    return e[0:1, :] / jnp.sum(e, axis=0, keepdims=True)


def _hgrn2(z_ref, lbl_ref, hgn_ref, br_ref, st_ref, st_save_ref, min_decay, *, tile, length, seg_len):
    L = length
    chunks = _chunks(tile, L)
    blocks_per_seg = seg_len // L
    lb = _forget_floor(lbl_ref)
    causal = _lower_tri(L)
    tri = jnp.where(causal, 1.0, 0.0).astype(BF16)
    for sg in range(tile // seg_len):
        for h in range(HEADS):
            st_save_ref[sg, h] = st_ref[sg, h]

    k, a_cum = [], []
    for rows in chunks:
        f = lb + (1.0 - lb) * _sigmoid(z_ref[rows, _HF:_HF + WIDTH])
        g_hi, g_lo = _split_bf16(jnp.log(f))
        a_cum.append(_dot(tri, g_hi) + _dot(tri, g_lo))
        k.append(1.0 - f)
        yield

    q_rel, k_rel, q_in, k_out, decay = [], [], [], [], []
    for c, rows in enumerate(chunks):
        a = a_cum[c]
        a_mid = a[L // 2 - 1:L // 2, :]
        a_end = a[L - 1:L, :]
        hq = z_ref[rows, _HQ:_HQ + WIDTH]
        qr = hq * _sigmoid(hq) * jnp.exp(a - a_mid)
        kr = k[c] * jnp.exp(a_mid - a)
        q_in.append((qr * jnp.exp(a_mid)).astype(BF16))
        k_out.append((kr * jnp.exp(a_end - a_mid)).astype(BF16))
        q_rel.append(qr.astype(BF16))
        k_rel.append(kr.astype(BF16))
        decay.append(jnp.exp(a_end))
        yield
    ends = a_cum[0][L - 1:L, :]
    for a in a_cum[1:]:
        ends = jnp.minimum(ends, a[L - 1:L, :])
    min_decay.append(jnp.min(ends))
    yield

    v = [z_ref[rows, _HI:_HI + WIDTH].astype(BF16) for rows in chunks]
    scores = [[jnp.where(causal, _dot_nt(q_rel[c][:, hs], k_rel[c][:, hs]), 0.0).astype(BF16)
               for hs in _HEAD_COLS] for c in range(len(chunks))]
    update = [[_dot_tn(v[c][:, hs], k_out[c][:, hs]) for hs in _HEAD_COLS] for c in range(len(chunks))]
    yield

    s_in = []
    for h, hs in enumerate(_HEAD_COLS):
        per_chunk = []
        for sg in range(tile // seg_len):
            s = st_ref[sg, h]
            for c in range(sg * blocks_per_seg, (sg + 1) * blocks_per_seg):
                per_chunk.append(s.astype(BF16))
                s = decay[c][:, hs] * s + update[c][h]
            st_ref[sg, h] = s
        s_in.append(per_chunk)
        yield

    hgn = hgn_ref[...]
    for c, rows in enumerate(chunks):
        gate = z_ref[rows, _HG:_HG + WIDTH]
        gate = gate * _sigmoid(gate)
        for h, hs in enumerate(_HEAD_COLS):
            o = _dot(scores[c][h], v[c][:, hs]) + _dot_nt(q_in[c][:, hs], s_in[h][c])
            br_ref[rows, hs] = (_rms(o, hgn[:, hs]) * gate[:, hs]).astype(BF16)
        yield


def _hgrn2_per_token(z_ref, lbl_ref, st_ref, st_save_ref, *, seg, seg_len):
    lb = _forget_floor(lbl_ref)
    st_ref = st_ref.at[seg]
    for h in range(HEADS):
        st_ref[h] = st_save_ref[seg, h]

    pad = 16
    first_row = lax.broadcasted_iota(jnp.int32, (pad, WIDTH), 0) == 0
    group = 8
    row_id = lax.broadcasted_iota(jnp.int32, (group, WIDTH), 0)

    def token_group(gi, carry):
        rows = pl.ds(pl.multiple_of(seg * seg_len + gi * group, group), group)
        hq = z_ref[rows, _HQ:_HQ + WIDTH]
        f = lb + (1.0 - lb) * _sigmoid(z_ref[rows, _HF:_HF + WIDTH])
        q = hq * _sigmoid(hq)
        k = 1.0 - f
        v = z_ref[rows, _HI:_HI + WIDTH]
        out = jnp.zeros((group, WIDTH), F32)
        for j in range(group):
            qj = jnp.where(first_row, q[j:j + 1, :], 0.0).astype(BF16)
            kj = jnp.where(first_row, k[j:j + 1, :], 0.0).astype(BF16)
            vj = jnp.where(first_row, v[j:j + 1, :], 0.0).astype(BF16)
            outs = []
            for h, hs in enumerate(_HEAD_COLS):
                s = f[j:j + 1, hs] * st_ref[h] + _dot_tn(vj[:, hs], kj[:, hs])
                st_ref[h] = s
                outs.append(_dot_nt(qj[:, hs], s.astype(BF16))[0:1, :])
            out = jnp.where(row_id == j, jnp.concatenate(outs, axis=-1), out)
        z_ref[rows, _HQ:_HQ + WIDTH] = out
        return carry

    lax.fori_loop(0, seg_len // group, token_group, 0)


def _redo_tile_per_token(x_ref, h_ref, z_ref, lbl_ref, hgn_ref, br_ref, sig_ref, wbr_ref, wout_ref,
                         st_ref, st_save_ref, *, tile, seg_len):
    for seg in range(tile // seg_len):
        _hgrn2_per_token(z_ref, lbl_ref, st_ref, st_save_ref, seg=seg, seg_len=seg_len)
    n_rows = min(tile, RECURRENCE_CHUNK)
    hgn = hgn_ref[...]

    def block(r, carry):
        rows = pl.ds(pl.multiple_of(r * n_rows, n_rows), n_rows)
        gate = z_ref[rows, _HG:_HG + WIDTH]
        gate = gate * _sigmoid(gate)
        for hs in _HEAD_COLS:
            o = z_ref[rows, _HQ + hs.start:_HQ + hs.stop]
            br_ref[rows, hs] = (_rms(o, hgn[:, hs]) * gate[:, hs]).astype(BF16)
        _drain(_merge(x_ref, h_ref, br_ref, sig_ref, wbr_ref, wout_ref, rows=rows))
        return carry

    lax.fori_loop(0, tile // n_rows, block, 0)


def _mlstm(z_ref, gcol_ref, grow_ref, mln_ref, br_ref, c_ref, n_ref, m_ref, *, tile, length, seg_len):
    L = length
    chunks = _chunks(tile, L)
    nc = len(chunks)
    seg_blocks = [range(sg * (seg_len // L), (sg + 1) * (seg_len // L)) for sg in range(tile // seg_len)]
    causal = _lower_tri(L)
    tri = jnp.where(causal, 1.0, 0.0).astype(BF16)
    tri_t = jnp.where(lax.broadcasted_iota(jnp.int32, (L, L), 0) <= lax.broadcasted_iota(jnp.int32, (L, L), 1),
                      1.0, 0.0).astype(BF16)
    lane = lax.broadcasted_iota(jnp.int32, (L, HEAD_DIM), 1)
    sub = lax.broadcasted_iota(jnp.int32, (grow_ref.shape[0], L), 0)

    g_c, g_r, b_c, b_r = [], [], [], []
    for rows in chunks:
        gc = gcol_ref[rows, :]
        gr = grow_ref[:, rows]
        fc_hi, fc_lo = _split_bf16(jnp.where(lane >= HEADS, _log_sigmoid(gc), 0.0))
        fr_hi, fr_lo = _split_bf16(jnp.where(sub >= HEADS, _log_sigmoid(gr), 0.0))
        g_c.append(gc)
        g_r.append(gr)
        b_c.append(_dot(tri, fc_hi) + _dot(tri, fc_lo))
        b_r.append(_dot(fr_hi, tri_t) + _dot(fr_lo, tri_t))
    yield

    blk = [[None] * HEADS for _ in range(nc)]
    for c, rows in enumerate(chunks):
        for h in range(HEADS):
            cols = slice(h * HEAD_DIM, (h + 1) * HEAD_DIM)
            i_col = g_c[c][:, h:h + 1]
            i_row = g_r[c][h:h + 1, :]
            b_col = b_c[c][:, HEADS + h:HEADS + h + 1]
            b_row = b_r[c][HEADS + h:HEADS + h + 1, :]
            b_last = b_row[:, L - 1:L]
            log_w = jnp.where(causal, b_col - b_row + i_row, -jnp.inf)
            q = z_ref[rows, _MQ + cols.start:_MQ + cols.stop]
            k = z_ref[rows, _MK + cols.start:_MK + cols.stop] * (HEAD_DIM ** -0.5)
            qb = q.astype(BF16)
            blk[c][h] = dict(
                b_col=b_col, b_last=b_last, log_w=log_w, q=q, k=k, qb=qb,
                w_max=jnp.max(log_w, axis=-1, keepdims=True),
                last_max=jnp.max(b_last - b_row + i_row, axis=-1, keepdims=True),
                log_last=b_last - b_col + i_col,
                qk=_dot_nt(qb, k.astype(BF16)),
                vb=z_ref[rows, _MV + cols.start:_MV + cols.stop].astype(BF16))
            if h % 2 == 1:
                yield

    for h in range(HEADS):
        for sg, blocks in enumerate(seg_blocks):
            m = m_ref[sg, h][:, 0:1]
            for c in blocks:
                d = blk[c][h]
                d["m_in"] = m
                m = jnp.maximum(d["b_last"] + m, d["last_max"])
                d["m_out"] = m
            m_ref[sg, h] = jnp.broadcast_to(m, (1, HEAD_DIM))
    yield

    for c in range(nc):
        for h in range(HEADS):
            d = blk[c][h]
            log_inter = d["b_col"] + d["m_in"]
            m_t = jnp.maximum(log_inter, d["w_max"])
            d["s"] = d["qk"] * jnp.exp(d["log_w"] - m_t)
            d["a"] = jnp.exp(log_inter - m_t)
            d["floor"] = jnp.exp(-m_t)
            kw = d["k"] * jnp.exp(d["log_last"] - d["m_out"])
            d["dec"] = jnp.exp(d["b_last"] + d["m_in"] - d["m_out"])
            d["c_upd"] = _dot_tn(kw.astype(BF16), d["vb"])
            d["n_upd"] = jnp.sum(kw, axis=0, keepdims=True)
            if h % 2 == 1:
                yield

    for h in range(HEADS):
        for sg, blocks in enumerate(seg_blocks):
            cm = c_ref[sg, h]
            n = n_ref[sg, h]
            for c in blocks:
                d = blk[c][h]
                d["c_in"] = cm.astype(BF16)
                d["n_in"] = n
                cm = d["dec"] * cm + d["c_upd"]
                n = d["dec"] * n + d["n_upd"]
            c_ref[sg, h] = cm
            n_ref[sg, h] = n
    yield

    mln = mln_ref[...]
    for c, rows in enumerate(chunks):
        for h, hs in enumerate(_HEAD_COLS):
            d = blk[c][h]
            num = _dot(d["s"].astype(BF16), d["vb"]) + d["a"] * _dot(d["qb"], d["c_in"])
            den = (jnp.sum(d["s"], axis=-1, keepdims=True)
                   + d["a"] * jnp.sum(d["q"] * d["n_in"], axis=-1, keepdims=True))
            hval = num / jnp.maximum(jnp.abs(den), d["floor"])
            og = z_ref[rows, _MO + hs.start:_MO + hs.stop]
            br_ref[rows, WIDTH + hs.start:WIDTH + hs.stop] = (
                _rms(hval, mln[:, hs]) * _sigmoid(og)).astype(BF16)
            if h % 2 == 1:
                yield


def _xattn(z_ref, mk_ref, mv_ref, br_ref, *, tile, seg_len):
    segs = _chunks(tile, seg_len)
    lg = [[_dot_nt(z_ref[rows, _XQ + hs.start:_XQ + hs.stop].astype(BF16), mk_ref[sg, :, hs]) * (HEAD_DIM ** -0.5)
           for sg, rows in enumerate(segs)] for hs in _HEAD_COLS]
    yield
    p = []
    for h in range(HEADS):
        per_seg = []
        for sg in range(len(segs)):
            e = jnp.exp(lg[h][sg] - jnp.max(lg[h][sg], axis=-1, keepdims=True))
            per_seg.append((e / jnp.sum(e, axis=-1, keepdims=True)).astype(BF16))
        p.append(per_seg)
        yield
    for h, hs in enumerate(_HEAD_COLS):
        for sg, rows in enumerate(segs):
            br_ref[rows, 2 * WIDTH + hs.start:2 * WIDTH + hs.stop] = _dot(
                p[h][sg], mv_ref[sg, :, hs]).astype(BF16)
    yield


def _merge(x_ref, h_ref, br_ref, sig_ref, wbr_ref, wout_ref, rows=slice(None)):
    d_model = x_ref.shape[-1]
    merged = None
    for i in range(N_BRANCH):
        y = (sig_ref[rows, i * d_model:(i + 1) * d_model]
             * _dot(br_ref[rows, i * WIDTH:(i + 1) * WIDTH], wbr_ref[i]))
        merged = y if merged is None else merged + y
        yield
    h_ref[rows, :] = x_ref[rows, :] + _dot(merged.astype(BF16), wout_ref[...])
    yield


_DONE = object()


def _rotate(*streams):
    lead = streams[0][0]
    while True:
        for gen, per_round in streams:
            for _ in range(per_round):
                if next(gen, _DONE) is _DONE and gen is lead:
                    return


def _drain(*gens):
    live = list(gens)
    while live:
        live = [gen for gen in live if next(gen, _DONE) is not _DONE]


def _mixer_kernel(xpair_ref, xnext_ref, s0_ref, c0_ref, n0_ref, m0_ref, mk_ref, mv_ref,
                  norm1_ref, wmain_ref, bmain_ref, wg_ref, wgt_ref, bgrow_ref, bgcol_ref,
                  lbl_ref, hgn_ref, mln_ref, wbr_ref, wout_ref,
                  h_ref, s_out_ref, c_out_ref, n_out_ref, m_out_ref,
                  uba_ref, ubb_ref, za_ref, zb_ref, siga_ref, sigb_ref, gca_ref, gcb_ref, gra_ref, grb_ref,
                  br_ref, st_ref, st_save_ref, c_ref, n_ref, m_ref,
                  *, tile, seg_len, hg_chunk, ml_chunk, tiles_per_row):
    g = pl.program_id(0)
    n_seg = tile // seg_len
    proj_w = (wmain_ref, bmain_ref, wg_ref, wgt_ref, bgrow_ref, bgcol_ref)
    ubs = (uba_ref, ubb_ref)
    bufs = ((za_ref, siga_ref, gca_ref, gra_ref), (zb_ref, sigb_ref, gcb_ref, grb_ref))

    @pl.when(g == 0)
    def _prologue():
        _drain(_normalise(xpair_ref.at[0], norm1_ref, ubs[0]))
        _drain(_project(ubs[0], *proj_w, *bufs[0]))
        _drain(_normalise(xpair_ref.at[1], norm1_ref, ubs[1]))

    for i in range(2):
        tile_in_row = (2 * g + i) % tiles_per_row if tiles_per_row > 1 else 0
        r0 = i * n_seg if tiles_per_row == 1 else 0

        def _load_state(r0=r0):
            for sg in range(n_seg):
                for h in range(HEADS):
                    st_ref[sg, h] = s0_ref[r0 + sg, h].T
                    c_ref[sg, h] = c0_ref[r0 + sg, h]
                    n_ref[sg, h] = n0_ref[r0 + sg, h:h + 1, :]
                    m_ref[sg, h] = m0_ref[r0 + sg, h:h + 1, :]

        def _store_state(r0=r0):
            for sg in range(n_seg):
                for h in range(HEADS):
                    s_out_ref[r0 + sg, h] = st_ref[sg, h].T
                    c_out_ref[r0 + sg, h] = c_ref[sg, h]
                    n_out_ref[r0 + sg, h:h + 1, :] = n_ref[sg, h]
                    m_out_ref[r0 + sg, h:h + 1, :] = m_ref[sg, h]

        if tiles_per_row == 1:
            _load_state()
        elif i == 0:
            pl.when(tile_in_row == 0)(_load_state)

        z_ref, sig_ref, gcol_ref, grow_ref = bufs[i]
        project_next = _project(ubs[1 - i], *proj_w, *bufs[1 - i])
        normalise_after = _normalise(xnext_ref.at[i], norm1_ref, ubs[i])
        min_decay = []
        hgrn2 = _hgrn2(z_ref, lbl_ref, hgn_ref, br_ref, st_ref, st_save_ref, min_decay,
                       tile=tile, length=hg_chunk, seg_len=seg_len)
        mlstm = _mlstm(z_ref, gcol_ref, grow_ref, mln_ref, br_ref, c_ref, n_ref, m_ref,
                       tile=tile, length=ml_chunk, seg_len=seg_len)
        xattn = _xattn(z_ref, mk_ref.at[r0:r0 + n_seg], mv_ref.at[r0:r0 + n_seg], br_ref,
                       tile=tile, seg_len=seg_len)
        _rotate((hgrn2, 1), (mlstm, 1), (xattn, 1), (project_next, 1))
        _drain(mlstm, xattn)
        _rotate((_merge(xpair_ref.at[i], h_ref.at[i], br_ref, sig_ref, wbr_ref, wout_ref), 1),
                (project_next, 3), (normalise_after, 1))
        _drain(project_next, normalise_after)

        @pl.when(min_decay[0] < -HGRN2_LOG_DECAY_RANGE)
        def _redo_hgrn2(i=i, z_ref=z_ref):
            _redo_tile_per_token(xpair_ref.at[i], h_ref.at[i], z_ref, lbl_ref, hgn_ref, br_ref, sig_ref,
                                 wbr_ref, wout_ref, st_ref, st_save_ref, tile=tile, seg_len=seg_len)

        if tiles_per_row == 1:
            _store_state()
        elif i == 1:
            pl.when(tile_in_row == tiles_per_row - 1)(_store_state)


def _mixer(x, s0, c0, n0, m0, mk, mv, p):
    B, T, D = x.shape
    if T % TOKEN_TILE == 0:
        tile = seg_len = TOKEN_TILE
    else:
        seg_len = T
        n_seg = max(n for n in range(1, B // 2 + 1) if B % (2 * n) == 0 and n * T <= TOKEN_TILE)
        tile = n_seg * T
    hg_chunk = RECURRENCE_CHUNK if seg_len % RECURRENCE_CHUNK == 0 else seg_len
    ml_chunk = MLSTM_CHUNK if seg_len % MLSTM_CHUNK == 0 else seg_len
    tiles_per_row = T // seg_len
    n_tiles = B * T // tile
    assert n_tiles % 2 == 0 and (tiles_per_row == 1 or tiles_per_row % 2 == 0)
    rows = 2 * (tile // seg_len) if tiles_per_row == 1 else 1
    steps_per_row_block = 1 if tiles_per_row == 1 else tiles_per_row // 2
    n_main = p["w_main"].shape[1]
    n_slots = p["lb_logits"].shape[0]
    M = mk.shape[1]
    gate_rows = p["w_gate_t"].shape[0]

    def full(shape):
        return pl.BlockSpec(shape, lambda g: (0,) * len(shape), pipeline_mode=pl.Buffered(1))

    def per_row(shape):
        return pl.BlockSpec((rows,) + shape, lambda g: (g // steps_per_row_block,) + (0,) * len(shape))

    state = per_row((HEADS, HEAD_DIM, HEAD_DIM))
    vec = per_row((HEADS, HEAD_DIM))
    m0_rep = jnp.broadcast_to(m0[:, :, None], (B, HEADS, HEAD_DIM))
    xt = x.reshape(n_tiles, tile, D)
    kern = functools.partial(_mixer_kernel, tile=tile, seg_len=seg_len, hg_chunk=hg_chunk, ml_chunk=ml_chunk,
                             tiles_per_row=tiles_per_row)
    n_seg = tile // seg_len
    outs = pl.pallas_call(
        kern,
        grid=(n_tiles // 2,),
        in_specs=[
            pl.BlockSpec((2, tile, D), lambda g: (g, 0, 0)),
            pl.BlockSpec((2, tile, D), lambda g: (jnp.minimum(g + 1, n_tiles // 2 - 1), 0, 0)),
            state, state, vec, vec, per_row((M, WIDTH)), per_row((M, WIDTH)),
            full((1, D)), full((D, n_main)), full((1, n_main)),
            full((D, HEAD_DIM)), full((gate_rows, D)), full((1, HEAD_DIM)), full((gate_rows, 1)),
            full((n_slots, WIDTH)), full((1, WIDTH)), full((1, WIDTH)),
            full((N_BRANCH, WIDTH, D)), full((D, D)),
        ],
        out_specs=[pl.BlockSpec((2, tile, D), lambda g: (g, 0, 0)), state, state, vec, vec],
        out_shape=[
            jax.ShapeDtypeStruct((n_tiles, tile, D), F32),
            jax.ShapeDtypeStruct((B, HEADS, HEAD_DIM, HEAD_DIM), F32),
            jax.ShapeDtypeStruct((B, HEADS, HEAD_DIM, HEAD_DIM), F32),
            jax.ShapeDtypeStruct((B, HEADS, HEAD_DIM), F32),
            jax.ShapeDtypeStruct((B, HEADS, HEAD_DIM), F32),
        ],
        scratch_shapes=[
            pltpu.VMEM((tile, D), BF16), pltpu.VMEM((tile, D), BF16),
            pltpu.VMEM((tile, _GL), F32), pltpu.VMEM((tile, _GL), F32),
            pltpu.VMEM((tile, n_main - _GL), BF16), pltpu.VMEM((tile, n_main - _GL), BF16),
            pltpu.VMEM((tile, HEAD_DIM), F32), pltpu.VMEM((tile, HEAD_DIM), F32),
            pltpu.VMEM((gate_rows, tile), F32), pltpu.VMEM((gate_rows, tile), F32),
            pltpu.VMEM((tile, N_BRANCH * WIDTH), BF16),
            pltpu.VMEM((n_seg, HEADS, HEAD_DIM, HEAD_DIM), F32),
            pltpu.VMEM((n_seg, HEADS, HEAD_DIM, HEAD_DIM), F32),
            pltpu.VMEM((n_seg, HEADS, HEAD_DIM, HEAD_DIM), F32),
            pltpu.VMEM((n_seg, HEADS, 1, HEAD_DIM), F32),
            pltpu.VMEM((n_seg, HEADS, 1, HEAD_DIM), F32),
        ],
        compiler_params=pltpu.CompilerParams(
            dimension_semantics=("arbitrary",), vmem_limit_bytes=VMEM_LIMIT_BYTES),
        name="mixer",
    )(xt, xt, s0, c0, n0, m0_rep, mk, mv,
      p["norm1"], p["w_main"], p["b_main"], p["w_gate"], p["w_gate_t"], p["b_gate_row"], p["b_gate_col"],
      p["lb_logits"], p["hg_norm"], p["ml_norm"], p["w_branch"], p["w_out"])
    return (outs[0].reshape(B, T, D),) + tuple(outs[1:])


_GELU_C = 0.7978845608028654


def _gelu_tanh_times(x, y):
    inner = x * (_GELU_C + (_GELU_C * 0.044715) * (x * x))
    return (0.5 + 0.5 * jnp.tanh(inner)) * (x * y)


def _ffn_kernel(h_ref, cv0_ref, norm2_ref, wup_ref, cw_ref, cb_ref, wdn_ref, fn_ref,
                y_ref, cv_out_ref, ug_ref, act_ref, hb_ref, *, n_seg, seg_len, n_sub, tiles_per_row, d_ff):
    t = pl.program_id(0) % tiles_per_row if tiles_per_row > 1 else 0
    last_t = tiles_per_row - 1
    head = 8
    keep = CONV_W - 1
    sub = n_seg * seg_len // n_sub
    if n_seg == 1:
        pieces = [[(slice(j * sub, (j + 1) * sub), 0, j * sub)] for j in range(n_sub)]
    else:
        pieces = [[(slice(s * seg_len, (s + 1) * seg_len), s, 0) for s in range(n_seg)]]

    def normalise(j):
        for r0 in range(j * sub, (j + 1) * sub, min(sub, 128)):
            rows = slice(r0, r0 + min(sub, 128))
            hb_ref[rows, :] = _rms(h_ref[rows, :], norm2_ref[...]).astype(BF16)
            yield

    def main(j):
        tile_rows = slice(j * sub, (j + 1) * sub)
        hb = hb_ref[tile_rows, :]
        for c0 in range(0, d_ff, 256):
            cols = slice(c0, c0 + 256)
            up_g = _dot(hb, wup_ref[:, cols])
            up_v = _dot(hb, wup_ref[:, d_ff + c0:d_ff + c0 + 256])
            for rows, s, off in pieces[j]:
                local = slice(rows.start - tile_rows.start, rows.stop - tile_rows.start)
                n = rows.stop - rows.start
                ug_ref[s, off + head:off + head + n, cols] = up_g[local, :]
                win = ug_ref[s, off:off + head + n, cols]
                conv = cb_ref[:, cols] + cw_ref[CONV_W - 1:CONV_W, cols] * win[head:, :]
                for k in range(1, CONV_W):
                    conv = conv + cw_ref[CONV_W - 1 - k:CONV_W - k, cols] * pltpu.roll(win, k, axis=0)[head:, :]
                act_ref[rows, cols] = _gelu_tanh_times(conv, up_v[local, :]).astype(BF16)
            yield
        y_ref[tile_rows, :] = h_ref[tile_rows, :] + _dot(act_ref[tile_rows, :], wdn_ref[...])
        yield

    def finalise(j):
        for r0 in range(j * sub, (j + 1) * sub, min(sub, 128)):
            rows = slice(r0, r0 + min(sub, 128))
            y_ref[rows, :] = _rms(y_ref[rows, :], fn_ref[...])
            yield

    def _first():
        for s in range(n_seg):
            ug_ref[s, 0:head - keep, :] = jnp.zeros((head - keep, d_ff), F32)
            ug_ref[s, head - keep:head, :] = cv0_ref[s]

    def _carry():
        for s in range(n_seg):
            ug_ref[s, head - keep:head, :] = ug_ref[s, head + seg_len - keep:head + seg_len, :]

    def _store():
        for s in range(n_seg):
            cv_out_ref[s] = ug_ref[s, head + seg_len - keep:head + seg_len, :]

    if tiles_per_row == 1:
        _first()
    else:
        pl.when(t == 0)(_first)
        pl.when(t > 0)(_carry)

    _drain(normalise(0))
    for j in range(n_sub):
        side = ([normalise(j + 1)] if j + 1 < n_sub else []) + ([finalise(j - 1)] if j > 0 else [])
        _rotate((main(j), 1), *[(gen, 1) for gen in side])
        _drain(*side)
    _drain(finalise(n_sub - 1))

    if tiles_per_row == 1:
        _store()
    else:
        pl.when(t == last_t)(_store)


def _ffn(h, cv0, p):
    B, T, D = h.shape
    d_ff = p["w_down"].shape[0]
    if T % FFN_TOKEN_TILE == 0:
        n_seg, seg_len = 1, FFN_TOKEN_TILE
    else:
        n_seg, seg_len = max(1, min(B, TOKEN_TILE // T)), T
    assert B % n_seg == 0 and seg_len % 8 == 0
    keep = CONV_W - 1
    rows = n_seg * seg_len
    tiles_per_row = T // seg_len
    n_sub = FFN_SUB_TILES if n_seg == 1 and seg_len % (FFN_SUB_TILES * 128) == 0 else 1

    def full(shape):
        return pl.BlockSpec(shape, lambda s: (0,) * len(shape), pipeline_mode=pl.Buffered(1))

    kern = functools.partial(_ffn_kernel, n_seg=n_seg, seg_len=seg_len, n_sub=n_sub,
                             tiles_per_row=tiles_per_row, d_ff=d_ff)
    y, cv = pl.pallas_call(
        kern,
        grid=(B * T // rows,),
        in_specs=[
            pl.BlockSpec((rows, D), lambda s: (s, 0)),
            pl.BlockSpec((n_seg, keep, d_ff), lambda s: (s // tiles_per_row, 0, 0)),
            full((1, D)), full((D, 2 * d_ff)), full((CONV_W, d_ff)), full((1, d_ff)),
            full((d_ff, D)), full((1, D)),
        ],
        out_specs=[pl.BlockSpec((rows, D), lambda s: (s, 0)),
                   pl.BlockSpec((n_seg, keep, d_ff), lambda s: (s // tiles_per_row, 0, 0))],
        out_shape=[jax.ShapeDtypeStruct((B * T, D), F32),
                   jax.ShapeDtypeStruct((B, keep, d_ff), F32)],
        scratch_shapes=[pltpu.VMEM((n_seg, 8 + seg_len, d_ff), F32),
                        pltpu.VMEM((rows, d_ff), BF16),
                        pltpu.VMEM((rows, D), BF16)],
        compiler_params=pltpu.CompilerParams(
            dimension_semantics=("arbitrary",), vmem_limit_bytes=VMEM_LIMIT_BYTES),
        name="conv_ffn",
    )(h.reshape(B * T, D), cv0, p["norm2"], p["w_up"], p["conv_w"], p["conv_b"], p["w_down"], p["final_norm"])
    return y.reshape(B, T, D), cv


def _regroup_kernel(w_ref, main_ref, gate_ref):
    g0 = 8 * WIDTH
    g1 = g0 + 2 * HEADS
    main_ref[:, :g0] = w_ref[:, :g0]
    main_ref[:, g0:] = w_ref[:, g1:]
    lane = lax.broadcasted_iota(jnp.int32, gate_ref.shape, 1)
    gate_ref[...] = jnp.where(lane < 2 * HEADS, w_ref[:, g0:g0 + HEAD_DIM], jnp.zeros((), BF16))


def _regroup_w_in(w_in, layer):
    _, D, n_in = w_in.shape
    n_main = n_in - 2 * HEADS
    rows = 128
    return pl.pallas_call(
        _regroup_kernel,
        grid=(D // rows,),
        in_specs=[pl.BlockSpec((None, rows, n_in), lambda r: (layer, r, 0))],
        out_specs=[pl.BlockSpec((rows, n_main), lambda r: (r, 0)),
                   pl.BlockSpec((rows, HEAD_DIM), lambda r: (r, 0))],
        out_shape=[jax.ShapeDtypeStruct((D, n_main), BF16), jax.ShapeDtypeStruct((D, HEAD_DIM), BF16)],
        compiler_params=pltpu.CompilerParams(dimension_semantics=("arbitrary",)),
        name="regroup_w_in",
    )(w_in.astype(BF16))


def _prepare(layer, norm1, w_in_all, b_in, ml_fgate_bias, hg_lb_logits, hg_norm, ml_norm, w_branch, w_out,
             norm2, w_up, ffn_conv_w, ffn_conv_b, w_down, final_norm):
    g0 = 8 * WIDTH
    g1 = g0 + 2 * HEADS
    w_main, w_gate = _regroup_w_in(w_in_all, layer)
    b_main = jnp.concatenate([b_in[:g0], b_in[g1:]])[None, :]
    gate_rows = 16
    w_gate_t = w_gate[:, :gate_rows].T
    b_gate = b_in[g0:g1] + jnp.concatenate([jnp.zeros((HEADS,), F32), ml_fgate_bias])
    return dict(
        norm1=norm1[None, :], w_main=w_main, b_main=b_main, w_gate=w_gate, w_gate_t=w_gate_t,
        b_gate_row=jnp.zeros((1, HEAD_DIM), F32).at[0, :2 * HEADS].set(b_gate),
        b_gate_col=jnp.zeros((gate_rows, 1), F32).at[:2 * HEADS, 0].set(b_gate),
        lb_logits=hg_lb_logits, hg_norm=hg_norm[None, :], ml_norm=ml_norm[None, :],
        w_branch=w_branch.astype(BF16), w_out=w_out.astype(BF16),
        norm2=norm2[None, :], w_up=w_up.astype(BF16), conv_w=ffn_conv_w, conv_b=ffn_conv_b[None, :],
        w_down=w_down.astype(BF16), final_norm=final_norm[None, :])


def _layer(x, s_hg, s_c, s_n, s_m, s_cv, mk, mv, p):
    h1, hg, c, n, m = _mixer(x, s_hg, s_c, s_n, s_m, mk, mv, p)
    y, cv = _ffn(h1, s_cv, p)
    return y, hg, c, n, m[:, :, 0], cv


def kernel(x_prompt, x_sample, state_hgrn, state_mlstm_C, state_mlstm_n, state_mlstm_m, state_ffn_conv, cache_mem_k, cache_mem_v, mem_prompt, norm1, w_in, b_in, ml_fgate_bias, hg_lb_logits, hg_norm, ml_norm, mem_norm, w_mem_kv, w_branch, w_out, norm2, w_up, ffn_conv_w, ffn_conv_b, w_down, final_norm):
    depth = norm1.shape[0]
    assert depth == 1, "single-layer encoder"
    assert hg_lb_logits.shape[0] == depth + 1
    Bp = x_prompt.shape[0]
    Bs = x_sample.shape[0]
    M = mem_prompt.shape[1]
    d_ff = w_down.shape[1]
    l = 0
    p = _prepare(l, norm1[l], w_in, b_in[l], ml_fgate_bias[l], hg_lb_logits, hg_norm[l], ml_norm[l],
                 w_branch[l], w_out[l], norm2[l], w_up[l], ffn_conv_w[l], ffn_conv_b[l], w_down[l], final_norm)

    mk_p, mv_p, mk_pb, mv_pb = _memory_kv(mem_prompt, mem_norm[l], w_mem_kv[l])
    zeros = lambda *s: jnp.zeros(s, F32)
    yp, hg_p, c_p, n_p, m_p, cv_p = _layer(
        x_prompt, zeros(Bp, HEADS, HEAD_DIM, HEAD_DIM), zeros(Bp, HEADS, HEAD_DIM, HEAD_DIM),
        zeros(Bp, HEADS, HEAD_DIM), zeros(Bp, HEADS), zeros(Bp, CONV_W - 1, d_ff), mk_pb, mv_pb, p)
    ys, hg_s, c_s, n_s, m_s, cv_s = _layer(
        x_sample, state_hgrn[l], state_mlstm_C[l], state_mlstm_n[l], state_mlstm_m[l], state_ffn_conv[l],
        cache_mem_k[l].reshape(Bs, M, WIDTH).astype(BF16), cache_mem_v[l].reshape(Bs, M, WIDTH).astype(BF16), p)

    return (yp, ys,
            hg_p[None], c_p[None], n_p[None], m_p[None],
            mk_p[None], mv_p[None], cv_p[None],
            hg_s[None], c_s[None], n_s[None], m_s[None], cv_s[None])
```

```python
import functools

import jax
import jax.numpy as jnp
from jax import lax
from jax.experimental import pallas as pl
from jax.experimental.pallas import tpu as pltpu

HEADS = 4
HEAD_DIM = 128
WIDTH = HEADS * HEAD_DIM
N_BRANCH = 3
CONV_W = 3
NORM_EPS = 1e-6
RECURRENCE_CHUNK = 64
MLSTM_CHUNK = 128
TOKEN_TILE = 256
FFN_TOKEN_TILE = 1024
FFN_SUB_TILES = 2
COL_GROUP = 256
FFN_COL_GROUP = 256
NORM_ROWS = 64
FFN_NORM_ROWS = 128
REGROUP_ROWS = 128
F32_SUBLANES = 8
BF16_SUBLANES = 16
VMEM_LIMIT_BYTES = 60 * 1024 * 1024
HGRN2_LOG_DECAY_RANGE = 60.0

F32 = jnp.float32
BF16 = jnp.bfloat16

_HQ, _HF, _HI, _HG = 0, WIDTH, 2 * WIDTH, 3 * WIDTH
_MQ, _MK, _MV, _MO = 4 * WIDTH, 5 * WIDTH, 6 * WIDTH, 7 * WIDTH
_XQ = 8 * WIDTH
_GL = 9 * WIDTH

_HEAD_COLS = tuple(slice(h * HEAD_DIM, (h + 1) * HEAD_DIM) for h in range(HEADS))


def _dot(a, b):
    return jnp.dot(a, b, preferred_element_type=F32)


def _dot_nt(a, b):
    return lax.dot_general(a, b, (((1,), (1,)), ((), ())), preferred_element_type=F32)


def _dot_tn(a, b):
    return lax.dot_general(a, b, (((0,), (0,)), ((), ())), preferred_element_type=F32)


def _sigmoid(x):
    return 1.0 / (1.0 + jnp.exp(-x))


def _log_sigmoid(x):
    return jnp.minimum(x, 0.0) - jnp.log(1.0 + jnp.exp(-jnp.abs(x)))


def _rms(x, g):
    return x * lax.rsqrt(jnp.mean(x * x, axis=-1, keepdims=True) + NORM_EPS) * g


def _split_bf16(x):
    hi = x.astype(BF16)
    lo = (x - hi.astype(F32)).astype(BF16)
    return hi, lo


def _lower_tri(n):
    r = lax.broadcasted_iota(jnp.int32, (n, n), 0)
    c = lax.broadcasted_iota(jnp.int32, (n, n), 1)
    return r >= c


def _chunks(tile, length):
    return [slice(c * length, (c + 1) * length) for c in range(tile // length)]


def _memkv_kernel(mem_ref, g_ref, w_ref, k_ref, v_ref, kb_ref, vb_ref):
    u = _rms(mem_ref[0], g_ref[...]).astype(BF16)
    kv = _dot(u, w_ref[...])
    k = kv[:, :WIDTH]
    v = kv[:, WIDTH:]
    for h, hs in enumerate(_HEAD_COLS):
        k_ref[0, :, h, :] = k[:, hs]
        v_ref[0, :, h, :] = v[:, hs]
    kb_ref[0] = k.astype(BF16)
    vb_ref[0] = v.astype(BF16)


def _memory_kv(mem, g, w):
    B, M, D = mem.shape
    full = lambda shape: pl.BlockSpec(shape, lambda b: (0,) * len(shape))
    per_b = lambda shape: pl.BlockSpec((1,) + shape, lambda b: (b, 0, 0))
    return pl.pallas_call(
        _memkv_kernel,
        grid=(B,),
        in_specs=[per_b((M, D)), full((1, D)), full((D, 2 * WIDTH))],
        out_specs=[pl.BlockSpec((1, M, HEADS, HEAD_DIM), lambda b: (b, 0, 0, 0))] * 2 + [per_b((M, WIDTH))] * 2,
        out_shape=[jax.ShapeDtypeStruct((B, M, HEADS, HEAD_DIM), F32)] * 2
        + [jax.ShapeDtypeStruct((B, M, WIDTH), BF16)] * 2,
        compiler_params=pltpu.CompilerParams(dimension_semantics=("arbitrary",)),
        name="memory_kv",
    )(mem, g.reshape(1, D), w.astype(BF16))


def _normalise(x_ref, norm1_ref, ub_ref):
    n_rows = min(x_ref.shape[0], NORM_ROWS)
    for r0 in range(0, x_ref.shape[0], n_rows):
        rows = slice(r0, r0 + n_rows)
        ub_ref[rows, :] = _rms(x_ref[rows, :], norm1_ref[...]).astype(BF16)
        yield


def _project(ub_ref, wmain_ref, bmain_ref, wg_ref, wgt_ref, bgrow_ref, bgcol_ref,
             z_ref, sig_ref, gcol_ref, grow_ref):
    ub = ub_ref[...]
    n_main = wmain_ref.shape[1]
    gcol_ref[...] = _dot(ub, wg_ref[...]) + bgrow_ref[...]
    grow_ref[...] = _dot_nt(wgt_ref[...], ub) + bgcol_ref[...]
    for c0 in range(0, n_main, COL_GROUP):
        cols = slice(c0, c0 + COL_GROUP)
        zc = _dot(ub, wmain_ref[:, cols]) + bmain_ref[:, cols]
        if c0 < _GL:
            z_ref[:, cols] = zc
        else:
            sig_ref[:, c0 - _GL:c0 - _GL + COL_GROUP] = _sigmoid(zc).astype(BF16)
        yield


def _forget_floor(lbl_ref):
    logits = lbl_ref[...]
    e = jnp.exp(logits - jnp.max(logits, axis=0, keepdims=True))
    return e[0:1, :] / jnp.sum(e, axis=0, keepdims=True)


def _hgrn2(z_ref, lbl_ref, hgn_ref, br_ref, st_ref, st_save_ref, min_decay, *, tile, length, seg_len):
    L = length
    chunks = _chunks(tile, L)
    blocks_per_seg = seg_len // L
    lb = _forget_floor(lbl_ref)
    causal = _lower_tri(L)
    tri = jnp.where(causal, 1.0, 0.0).astype(BF16)
    for sg in range(tile // seg_len):
        for h in range(HEADS):
            st_save_ref[sg, h] = st_ref[sg, h]

    k, a_cum = [], []
    for rows in chunks:
        f = lb + (1.0 - lb) * _sigmoid(z_ref[rows, _HF:_HF + WIDTH])
        g_hi, g_lo = _split_bf16(jnp.log(f))
        a_cum.append(_dot(tri, g_hi) + _dot(tri, g_lo))
        k.append(1.0 - f)
        yield

    q_rel, k_rel, q_in, k_out, decay = [], [], [], [], []
    for c, rows in enumerate(chunks):
        a = a_cum[c]
        a_mid = a[L // 2 - 1:L // 2, :]
        a_end = a[L - 1:L, :]
        hq = z_ref[rows, _HQ:_HQ + WIDTH]
        qr = hq * _sigmoid(hq) * jnp.exp(a - a_mid)
        kr = k[c] * jnp.exp(a_mid - a)
        q_in.append((qr * jnp.exp(a_mid)).astype(BF16))
        k_out.append((kr * jnp.exp(a_end - a_mid)).astype(BF16))
        q_rel.append(qr.astype(BF16))
        k_rel.append(kr.astype(BF16))
        decay.append(jnp.exp(a_end))
        yield
    ends = a_cum[0][L - 1:L, :]
    for a in a_cum[1:]:
        ends = jnp.minimum(ends, a[L - 1:L, :])
    min_decay.append(jnp.min(ends))
    yield

    v = [z_ref[rows, _HI:_HI + WIDTH].astype(BF16) for rows in chunks]
    scores = [[jnp.where(causal, _dot_nt(q_rel[c][:, hs], k_rel[c][:, hs]), 0.0).astype(BF16)
               for hs in _HEAD_COLS] for c in range(len(chunks))]
    update = [[_dot_tn(v[c][:, hs], k_out[c][:, hs]) for hs in _HEAD_COLS] for c in range(len(chunks))]
    yield

    s_in = []
    for h, hs in enumerate(_HEAD_COLS):
        per_chunk = []
        for sg in range(tile // seg_len):
            s = st_ref[sg, h]
            for c in range(sg * blocks_per_seg, (sg + 1) * blocks_per_seg):
                per_chunk.append(s.astype(BF16))
                s = decay[c][:, hs] * s + update[c][h]
            st_ref[sg, h] = s
        s_in.append(per_chunk)
        yield

    hgn = hgn_ref[...]
    for c, rows in enumerate(chunks):
        gate = z_ref[rows, _HG:_HG + WIDTH]
        gate = gate * _sigmoid(gate)
        for h, hs in enumerate(_HEAD_COLS):
            o = _dot(scores[c][h], v[c][:, hs]) + _dot_nt(q_in[c][:, hs], s_in[h][c])
            br_ref[rows, hs] = (_rms(o, hgn[:, hs]) * gate[:, hs]).astype(BF16)
        yield


def _hgrn2_per_token(z_ref, lbl_ref, st_ref, st_save_ref, *, seg, seg_len):
    lb = _forget_floor(lbl_ref)
    st_ref = st_ref.at[seg]
    for h in range(HEADS):
        st_ref[h] = st_save_ref[seg, h]

    pad = BF16_SUBLANES
    first_row = lax.broadcasted_iota(jnp.int32, (pad, WIDTH), 0) == 0
    group = F32_SUBLANES
    row_id = lax.broadcasted_iota(jnp.int32, (group, WIDTH), 0)

    def token_group(gi, carry):
        rows = pl.ds(pl.multiple_of(seg * seg_len + gi * group, group), group)
        hq = z_ref[rows, _HQ:_HQ + WIDTH]
        f = lb + (1.0 - lb) * _sigmoid(z_ref[rows, _HF:_HF + WIDTH])
        q = hq * _sigmoid(hq)
        k = 1.0 - f
        v = z_ref[rows, _HI:_HI + WIDTH]
        out = jnp.zeros((group, WIDTH), F32)
        for j in range(group):
            qj = jnp.where(first_row, q[j:j + 1, :], 0.0).astype(BF16)
            kj = jnp.where(first_row, k[j:j + 1, :], 0.0).astype(BF16)
            vj = jnp.where(first_row, v[j:j + 1, :], 0.0).astype(BF16)
            outs = []
            for h, hs in enumerate(_HEAD_COLS):
                s = f[j:j + 1, hs] * st_ref[h] + _dot_tn(vj[:, hs], kj[:, hs])
                st_ref[h] = s
                outs.append(_dot_nt(qj[:, hs], s.astype(BF16))[0:1, :])
            out = jnp.where(row_id == j, jnp.concatenate(outs, axis=-1), out)
        z_ref[rows, _HQ:_HQ + WIDTH] = out
        return carry

    lax.fori_loop(0, seg_len // group, token_group, 0)


def _redo_tile_per_token(x_ref, h_ref, z_ref, lbl_ref, hgn_ref, br_ref, sig_ref, wbr_ref, wout_ref,
                         st_ref, st_save_ref, *, tile, seg_len):
    for seg in range(tile // seg_len):
        _hgrn2_per_token(z_ref, lbl_ref, st_ref, st_save_ref, seg=seg, seg_len=seg_len)
    n_rows = min(tile, RECURRENCE_CHUNK)
    hgn = hgn_ref[...]

    def block(r, carry):
        rows = pl.ds(pl.multiple_of(r * n_rows, n_rows), n_rows)
        gate = z_ref[rows, _HG:_HG + WIDTH]
        gate = gate * _sigmoid(gate)
        for hs in _HEAD_COLS:
            o = z_ref[rows, _HQ + hs.start:_HQ + hs.stop]
            br_ref[rows, hs] = (_rms(o, hgn[:, hs]) * gate[:, hs]).astype(BF16)
        _drain(_merge(x_ref, h_ref, br_ref, sig_ref, wbr_ref, wout_ref, rows=rows))
        return carry

    lax.fori_loop(0, tile // n_rows, block, 0)


def _mlstm(z_ref, gcol_ref, grow_ref, mln_ref, br_ref, c_ref, n_ref, m_ref, *, tile, length, seg_len):
    L = length
    chunks = _chunks(tile, L)
    nc = len(chunks)
    seg_blocks = [range(sg * (seg_len // L), (sg + 1) * (seg_len // L)) for sg in range(tile // seg_len)]
    causal = _lower_tri(L)
    tri = jnp.where(causal, 1.0, 0.0).astype(BF16)
    tri_t = jnp.where(lax.broadcasted_iota(jnp.int32, (L, L), 0) <= lax.broadcasted_iota(jnp.int32, (L, L), 1),
                      1.0, 0.0).astype(BF16)
    lane = lax.broadcasted_iota(jnp.int32, (L, HEAD_DIM), 1)
    sub = lax.broadcasted_iota(jnp.int32, (grow_ref.shape[0], L), 0)

    g_c, g_r, b_c, b_r = [], [], [], []
    for rows in chunks:
        gc = gcol_ref[rows, :]
        gr = grow_ref[:, rows]
        fc_hi, fc_lo = _split_bf16(jnp.where(lane >= HEADS, _log_sigmoid(gc), 0.0))
        fr_hi, fr_lo = _split_bf16(jnp.where(sub >= HEADS, _log_sigmoid(gr), 0.0))
        g_c.append(gc)
        g_r.append(gr)
        b_c.append(_dot(tri, fc_hi) + _dot(tri, fc_lo))
        b_r.append(_dot(fr_hi, tri_t) + _dot(fr_lo, tri_t))
    yield

    blk = [[None] * HEADS for _ in range(nc)]
    for c, rows in enumerate(chunks):
        for h in range(HEADS):
            cols = slice(h * HEAD_DIM, (h + 1) * HEAD_DIM)
            i_col = g_c[c][:, h:h + 1]
            i_row = g_r[c][h:h + 1, :]
            b_col = b_c[c][:, HEADS + h:HEADS + h + 1]
            b_row = b_r[c][HEADS + h:HEADS + h + 1, :]
            b_last = b_row[:, L - 1:L]
            log_w = jnp.where(causal, b_col - b_row + i_row, -jnp.inf)
            q = z_ref[rows, _MQ + cols.start:_MQ + cols.stop]
            k = z_ref[rows, _MK + cols.start:_MK + cols.stop] * (HEAD_DIM ** -0.5)
            qb = q.astype(BF16)
            blk[c][h] = dict(
                b_col=b_col, b_last=b_last, log_w=log_w, q=q, k=k, qb=qb,
                w_max=jnp.max(log_w, axis=-1, keepdims=True),
                last_max=jnp.max(b_last - b_row + i_row, axis=-1, keepdims=True),
                log_last=b_last - b_col + i_col,
                qk=_dot_nt(qb, k.astype(BF16)),
                vb=z_ref[rows, _MV + cols.start:_MV + cols.stop].astype(BF16))
            if h % 2 == 1:
                yield

    for h in range(HEADS):
        for sg, blocks in enumerate(seg_blocks):
            m = m_ref[sg, h][:, 0:1]
            for c in blocks:
                d = blk[c][h]
                d["m_in"] = m
                m = jnp.maximum(d["b_last"] + m, d["last_max"])
                d["m_out"] = m
            m_ref[sg, h] = jnp.broadcast_to(m, (1, HEAD_DIM))
    yield

    for c in range(nc):
        for h in range(HEADS):
            d = blk[c][h]
            log_inter = d["b_col"] + d["m_in"]
            m_t = jnp.maximum(log_inter, d["w_max"])
            d["s"] = d["qk"] * jnp.exp(d["log_w"] - m_t)
            d["a"] = jnp.exp(log_inter - m_t)
            d["floor"] = jnp.exp(-m_t)
            kw = d["k"] * jnp.exp(d["log_last"] - d["m_out"])
            d["dec"] = jnp.exp(d["b_last"] + d["m_in"] - d["m_out"])
            d["c_upd"] = _dot_tn(kw.astype(BF16), d["vb"])
            d["n_upd"] = jnp.sum(kw, axis=0, keepdims=True)
            if h % 2 == 1:
                yield

    for h in range(HEADS):
        for sg, blocks in enumerate(seg_blocks):
            cm = c_ref[sg, h]
            n = n_ref[sg, h]
            for c in blocks:
                d = blk[c][h]
                d["c_in"] = cm.astype(BF16)
                d["n_in"] = n
                cm = d["dec"] * cm + d["c_upd"]
                n = d["dec"] * n + d["n_upd"]
            c_ref[sg, h] = cm
            n_ref[sg, h] = n
    yield

    mln = mln_ref[...]
    for c, rows in enumerate(chunks):
        for h, hs in enumerate(_HEAD_COLS):
            d = blk[c][h]
            num = _dot(d["s"].astype(BF16), d["vb"]) + d["a"] * _dot(d["qb"], d["c_in"])
            den = (jnp.sum(d["s"], axis=-1, keepdims=True)
                   + d["a"] * jnp.sum(d["q"] * d["n_in"], axis=-1, keepdims=True))
            hval = num / jnp.maximum(jnp.abs(den), d["floor"])
            og = z_ref[rows, _MO + hs.start:_MO + hs.stop]
            br_ref[rows, WIDTH + hs.start:WIDTH + hs.stop] = (
                _rms(hval, mln[:, hs]) * _sigmoid(og)).astype(BF16)
            if h % 2 == 1:
                yield


def _xattn(z_ref, mk_ref, mv_ref, br_ref, *, tile, seg_len):
    segs = _chunks(tile, seg_len)
    lg = [[_dot_nt(z_ref[rows, _XQ + hs.start:_XQ + hs.stop].astype(BF16), mk_ref[sg, :, hs]) * (HEAD_DIM ** -0.5)
           for sg, rows in enumerate(segs)] for hs in _HEAD_COLS]
    yield
    p = []
    for h in range(HEADS):
        per_seg = []
        for sg in range(len(segs)):
            e = jnp.exp(lg[h][sg] - jnp.max(lg[h][sg], axis=-1, keepdims=True))
            per_seg.append((e / jnp.sum(e, axis=-1, keepdims=True)).astype(BF16))
        p.append(per_seg)
        yield
    for h, hs in enumerate(_HEAD_COLS):
        for sg, rows in enumerate(segs):
            br_ref[rows, 2 * WIDTH + hs.start:2 * WIDTH + hs.stop] = _dot(
                p[h][sg], mv_ref[sg, :, hs]).astype(BF16)
    yield


def _merge(x_ref, h_ref, br_ref, sig_ref, wbr_ref, wout_ref, rows=slice(None)):
    d_model = x_ref.shape[-1]
    merged = None
    for i in range(N_BRANCH):
        y = (sig_ref[rows, i * d_model:(i + 1) * d_model]
             * _dot(br_ref[rows, i * WIDTH:(i + 1) * WIDTH], wbr_ref[i]))
        merged = y if merged is None else merged + y
        yield
    h_ref[rows, :] = x_ref[rows, :] + _dot(merged.astype(BF16), wout_ref[...])
    yield


_DONE = object()


def _rotate(*streams):
    lead = streams[0][0]
    while True:
        for gen, per_round in streams:
            for _ in range(per_round):
                if next(gen, _DONE) is _DONE and gen is lead:
                    return


def _drain(*gens):
    live = list(gens)
    while live:
        live = [gen for gen in live if next(gen, _DONE) is not _DONE]


def _mixer_kernel(xpair_ref, xnext_ref, s0_ref, c0_ref, n0_ref, m0_ref, mk_ref, mv_ref,
                  norm1_ref, wmain_ref, bmain_ref, wg_ref, wgt_ref, bgrow_ref, bgcol_ref,
                  lbl_ref, hgn_ref, mln_ref, wbr_ref, wout_ref,
                  h_ref, s_out_ref, c_out_ref, n_out_ref, m_out_ref,
                  uba_ref, ubb_ref, za_ref, zb_ref, siga_ref, sigb_ref, gca_ref, gcb_ref, gra_ref, grb_ref,
                  br_ref, st_ref, st_save_ref, c_ref, n_ref, m_ref,
                  *, tile, seg_len, hg_chunk, ml_chunk, tiles_per_row):
    g = pl.program_id(0)
    n_seg = tile // seg_len
    proj_w = (wmain_ref, bmain_ref, wg_ref, wgt_ref, bgrow_ref, bgcol_ref)
    ubs = (uba_ref, ubb_ref)
    bufs = ((za_ref, siga_ref, gca_ref, gra_ref), (zb_ref, sigb_ref, gcb_ref, grb_ref))

    @pl.when(g == 0)
    def _prologue():
        _drain(_normalise(xpair_ref.at[0], norm1_ref, ubs[0]))
        _drain(_project(ubs[0], *proj_w, *bufs[0]))
        _drain(_normalise(xpair_ref.at[1], norm1_ref, ubs[1]))

    for i in range(2):
        tile_in_row = (2 * g + i) % tiles_per_row if tiles_per_row > 1 else 0
        r0 = i * n_seg if tiles_per_row == 1 else 0

        def _load_state(r0=r0):
            for sg in range(n_seg):
                for h in range(HEADS):
                    st_ref[sg, h] = s0_ref[r0 + sg, h].T
                    c_ref[sg, h] = c0_ref[r0 + sg, h]
                    n_ref[sg, h] = n0_ref[r0 + sg, h:h + 1, :]
                    m_ref[sg, h] = m0_ref[r0 + sg, h:h + 1, :]

        def _store_state(r0=r0):
            for sg in range(n_seg):
                for h in range(HEADS):
                    s_out_ref[r0 + sg, h] = st_ref[sg, h].T
                    c_out_ref[r0 + sg, h] = c_ref[sg, h]
                    n_out_ref[r0 + sg, h:h + 1, :] = n_ref[sg, h]
                    m_out_ref[r0 + sg, h:h + 1, :] = m_ref[sg, h]

        if tiles_per_row == 1:
            _load_state()
        elif i == 0:
            pl.when(tile_in_row == 0)(_load_state)

        z_ref, sig_ref, gcol_ref, grow_ref = bufs[i]
        project_next = _project(ubs[1 - i], *proj_w, *bufs[1 - i])
        normalise_after = _normalise(xnext_ref.at[i], norm1_ref, ubs[i])
        min_decay = []
        hgrn2 = _hgrn2(z_ref, lbl_ref, hgn_ref, br_ref, st_ref, st_save_ref, min_decay,
                       tile=tile, length=hg_chunk, seg_len=seg_len)
        mlstm = _mlstm(z_ref, gcol_ref, grow_ref, mln_ref, br_ref, c_ref, n_ref, m_ref,
                       tile=tile, length=ml_chunk, seg_len=seg_len)
        xattn = _xattn(z_ref, mk_ref.at[r0:r0 + n_seg], mv_ref.at[r0:r0 + n_seg], br_ref,
                       tile=tile, seg_len=seg_len)
        _rotate((hgrn2, 1), (mlstm, 1), (xattn, 1), (project_next, 1))
        _drain(mlstm, xattn)
        _rotate((_merge(xpair_ref.at[i], h_ref.at[i], br_ref, sig_ref, wbr_ref, wout_ref), 1),
                (project_next, 3), (normalise_after, 1))
        _drain(project_next, normalise_after)

        @pl.when(min_decay[0] < -HGRN2_LOG_DECAY_RANGE)
        def _redo_hgrn2(i=i, z_ref=z_ref):
            _redo_tile_per_token(xpair_ref.at[i], h_ref.at[i], z_ref, lbl_ref, hgn_ref, br_ref, sig_ref,
                                 wbr_ref, wout_ref, st_ref, st_save_ref, tile=tile, seg_len=seg_len)

        if tiles_per_row == 1:
            _store_state()
        elif i == 1:
            pl.when(tile_in_row == tiles_per_row - 1)(_store_state)


def _mixer(x, s0, c0, n0, m0, mk, mv, p):
    B, T, D = x.shape
    if T % TOKEN_TILE == 0:
        tile = seg_len = TOKEN_TILE
    else:
        seg_len = T
        n_seg = max(n for n in range(1, B // 2 + 1) if B % (2 * n) == 0 and n * T <= TOKEN_TILE)
        tile = n_seg * T
    hg_chunk = RECURRENCE_CHUNK if seg_len % RECURRENCE_CHUNK == 0 else seg_len
    ml_chunk = MLSTM_CHUNK if seg_len % MLSTM_CHUNK == 0 else seg_len
    tiles_per_row = T // seg_len
    n_tiles = B * T // tile
    assert n_tiles % 2 == 0 and (tiles_per_row == 1 or tiles_per_row % 2 == 0)
    rows = 2 * (tile // seg_len) if tiles_per_row == 1 else 1
    steps_per_row_block = 1 if tiles_per_row == 1 else tiles_per_row // 2
    n_main = p["w_main"].shape[1]
    n_slots = p["lb_logits"].shape[0]
    M = mk.shape[1]
    gate_rows = p["w_gate_t"].shape[0]

    def full(shape):
        return pl.BlockSpec(shape, lambda g: (0,) * len(shape), pipeline_mode=pl.Buffered(1))

    def per_row(shape):
        return pl.BlockSpec((rows,) + shape, lambda g: (g // steps_per_row_block,) + (0,) * len(shape))

    state = per_row((HEADS, HEAD_DIM, HEAD_DIM))
    vec = per_row((HEADS, HEAD_DIM))
    m0_rep = jnp.broadcast_to(m0[:, :, None], (B, HEADS, HEAD_DIM))
    xt = x.reshape(n_tiles, tile, D)
    kern = functools.partial(_mixer_kernel, tile=tile, seg_len=seg_len, hg_chunk=hg_chunk, ml_chunk=ml_chunk,
                             tiles_per_row=tiles_per_row)
    n_seg = tile // seg_len
    outs = pl.pallas_call(
        kern,
        grid=(n_tiles // 2,),
        in_specs=[
            pl.BlockSpec((2, tile, D), lambda g: (g, 0, 0)),
            pl.BlockSpec((2, tile, D), lambda g: (jnp.minimum(g + 1, n_tiles // 2 - 1), 0, 0)),
            state, state, vec, vec, per_row((M, WIDTH)), per_row((M, WIDTH)),
            full((1, D)), full((D, n_main)), full((1, n_main)),
            full((D, HEAD_DIM)), full((gate_rows, D)), full((1, HEAD_DIM)), full((gate_rows, 1)),
            full((n_slots, WIDTH)), full((1, WIDTH)), full((1, WIDTH)),
            full((N_BRANCH, WIDTH, D)), full((D, D)),
        ],
        out_specs=[pl.BlockSpec((2, tile, D), lambda g: (g, 0, 0)), state, state, vec, vec],
        out_shape=[
            jax.ShapeDtypeStruct((n_tiles, tile, D), F32),
            jax.ShapeDtypeStruct((B, HEADS, HEAD_DIM, HEAD_DIM), F32),
            jax.ShapeDtypeStruct((B, HEADS, HEAD_DIM, HEAD_DIM), F32),
            jax.ShapeDtypeStruct((B, HEADS, HEAD_DIM), F32),
            jax.ShapeDtypeStruct((B, HEADS, HEAD_DIM), F32),
        ],
        scratch_shapes=[
            pltpu.VMEM((tile, D), BF16), pltpu.VMEM((tile, D), BF16),
            pltpu.VMEM((tile, _GL), F32), pltpu.VMEM((tile, _GL), F32),
            pltpu.VMEM((tile, n_main - _GL), BF16), pltpu.VMEM((tile, n_main - _GL), BF16),
            pltpu.VMEM((tile, HEAD_DIM), F32), pltpu.VMEM((tile, HEAD_DIM), F32),
            pltpu.VMEM((gate_rows, tile), F32), pltpu.VMEM((gate_rows, tile), F32),
            pltpu.VMEM((tile, N_BRANCH * WIDTH), BF16),
            pltpu.VMEM((n_seg, HEADS, HEAD_DIM, HEAD_DIM), F32),
            pltpu.VMEM((n_seg, HEADS, HEAD_DIM, HEAD_DIM), F32),
            pltpu.VMEM((n_seg, HEADS, HEAD_DIM, HEAD_DIM), F32),
            pltpu.VMEM((n_seg, HEADS, 1, HEAD_DIM), F32),
            pltpu.VMEM((n_seg, HEADS, 1, HEAD_DIM), F32),
        ],
        compiler_params=pltpu.CompilerParams(
            dimension_semantics=("arbitrary",), vmem_limit_bytes=VMEM_LIMIT_BYTES),
        name="mixer",
    )(xt, xt, s0, c0, n0, m0_rep, mk, mv,
      p["norm1"], p["w_main"], p["b_main"], p["w_gate"], p["w_gate_t"], p["b_gate_row"], p["b_gate_col"],
      p["lb_logits"], p["hg_norm"], p["ml_norm"], p["w_branch"], p["w_out"])
    return (outs[0].reshape(B, T, D),) + tuple(outs[1:])


_GELU_C = 0.7978845608028654


def _gelu_tanh_times(x, y):
    inner = x * (_GELU_C + (_GELU_C * 0.044715) * (x * x))
    return (0.5 + 0.5 * jnp.tanh(inner)) * (x * y)


def _ffn_kernel(h_ref, cv0_ref, norm2_ref, wup_ref, cw_ref, cb_ref, wdn_ref, fn_ref,
                y_ref, cv_out_ref, ug_ref, act_ref, hb_ref, *, n_seg, seg_len, n_sub, tiles_per_row, d_ff):
    t = pl.program_id(0) % tiles_per_row if tiles_per_row > 1 else 0
    last_t = tiles_per_row - 1
    head = F32_SUBLANES
    keep = CONV_W - 1
    sub = n_seg * seg_len // n_sub
    if n_seg == 1:
        pieces = [[(slice(j * sub, (j + 1) * sub), 0, j * sub)] for j in range(n_sub)]
    else:
        pieces = [[(slice(s * seg_len, (s + 1) * seg_len), s, 0) for s in range(n_seg)]]

    def normalise(j):
        for r0 in range(j * sub, (j + 1) * sub, min(sub, FFN_NORM_ROWS)):
            rows = slice(r0, r0 + min(sub, FFN_NORM_ROWS))
            hb_ref[rows, :] = _rms(h_ref[rows, :], norm2_ref[...]).astype(BF16)
            yield

    def main(j):
        tile_rows = slice(j * sub, (j + 1) * sub)
        hb = hb_ref[tile_rows, :]
        for c0 in range(0, d_ff, FFN_COL_GROUP):
            cols = slice(c0, c0 + FFN_COL_GROUP)
            up_g = _dot(hb, wup_ref[:, cols])
            up_v = _dot(hb, wup_ref[:, d_ff + c0:d_ff + c0 + FFN_COL_GROUP])
            for rows, s, off in pieces[j]:
                local = slice(rows.start - tile_rows.start, rows.stop - tile_rows.start)
                n = rows.stop - rows.start
                ug_ref[s, off + head:off + head + n, cols] = up_g[local, :]
                win = ug_ref[s, off:off + head + n, cols]
                conv = cb_ref[:, cols] + cw_ref[CONV_W - 1:CONV_W, cols] * win[head:, :]
                for k in range(1, CONV_W):
                    conv = conv + cw_ref[CONV_W - 1 - k:CONV_W - k, cols] * pltpu.roll(win, k, axis=0)[head:, :]
                act_ref[rows, cols] = _gelu_tanh_times(conv, up_v[local, :]).astype(BF16)
            yield
        y_ref[tile_rows, :] = h_ref[tile_rows, :] + _dot(act_ref[tile_rows, :], wdn_ref[...])
        yield

    def finalise(j):
        for r0 in range(j * sub, (j + 1) * sub, min(sub, FFN_NORM_ROWS)):
            rows = slice(r0, r0 + min(sub, FFN_NORM_ROWS))
            y_ref[rows, :] = _rms(y_ref[rows, :], fn_ref[...])
            yield

    def _first():
        for s in range(n_seg):
            ug_ref[s, 0:head - keep, :] = jnp.zeros((head - keep, d_ff), F32)
            ug_ref[s, head - keep:head, :] = cv0_ref[s]

    def _carry():
        for s in range(n_seg):
            ug_ref[s, head - keep:head, :] = ug_ref[s, head + seg_len - keep:head + seg_len, :]

    def _store():
        for s in range(n_seg):
            cv_out_ref[s] = ug_ref[s, head + seg_len - keep:head + seg_len, :]

    if tiles_per_row == 1:
        _first()
    else:
        pl.when(t == 0)(_first)
        pl.when(t > 0)(_carry)

    _drain(normalise(0))
    for j in range(n_sub):
        side = ([normalise(j + 1)] if j + 1 < n_sub else []) + ([finalise(j - 1)] if j > 0 else [])
        _rotate((main(j), 1), *[(gen, 1) for gen in side])
        _drain(*side)
    _drain(finalise(n_sub - 1))

    if tiles_per_row == 1:
        _store()
    else:
        pl.when(t == last_t)(_store)


def _ffn(h, cv0, p):
    B, T, D = h.shape
    d_ff = p["w_down"].shape[0]
    if T % FFN_TOKEN_TILE == 0:
        n_seg, seg_len = 1, FFN_TOKEN_TILE
    else:
        n_seg, seg_len = max(1, min(B, TOKEN_TILE // T)), T
    assert B % n_seg == 0 and seg_len % F32_SUBLANES == 0
    keep = CONV_W - 1
    rows = n_seg * seg_len
    tiles_per_row = T // seg_len
    n_sub = FFN_SUB_TILES if n_seg == 1 and seg_len % (FFN_SUB_TILES * 128) == 0 else 1

    def full(shape):
        return pl.BlockSpec(shape, lambda s: (0,) * len(shape), pipeline_mode=pl.Buffered(1))

    kern = functools.partial(_ffn_kernel, n_seg=n_seg, seg_len=seg_len, n_sub=n_sub,
                             tiles_per_row=tiles_per_row, d_ff=d_ff)
    y, cv = pl.pallas_call(
        kern,
        grid=(B * T // rows,),
        in_specs=[
            pl.BlockSpec((rows, D), lambda s: (s, 0)),
            pl.BlockSpec((n_seg, keep, d_ff), lambda s: (s // tiles_per_row, 0, 0)),
            full((1, D)), full((D, 2 * d_ff)), full((CONV_W, d_ff)), full((1, d_ff)),
            full((d_ff, D)), full((1, D)),
        ],
        out_specs=[pl.BlockSpec((rows, D), lambda s: (s, 0)),
                   pl.BlockSpec((n_seg, keep, d_ff), lambda s: (s // tiles_per_row, 0, 0))],
        out_shape=[jax.ShapeDtypeStruct((B * T, D), F32),
                   jax.ShapeDtypeStruct((B, keep, d_ff), F32)],
        scratch_shapes=[pltpu.VMEM((n_seg, F32_SUBLANES + seg_len, d_ff), F32),
                        pltpu.VMEM((rows, d_ff), BF16),
                        pltpu.VMEM((rows, D), BF16)],
        compiler_params=pltpu.CompilerParams(
            dimension_semantics=("arbitrary",), vmem_limit_bytes=VMEM_LIMIT_BYTES),
        name="conv_ffn",
    )(h.reshape(B * T, D), cv0, p["norm2"], p["w_up"], p["conv_w"], p["conv_b"], p["w_down"], p["final_norm"])
    return y.reshape(B, T, D), cv


def _regroup_kernel(w_ref, main_ref, gate_ref):
    g0 = 8 * WIDTH
    g1 = g0 + 2 * HEADS
    main_ref[:, :g0] = w_ref[:, :g0].astype(BF16)
    main_ref[:, g0:] = w_ref[:, g1:].astype(BF16)
    lane = lax.broadcasted_iota(jnp.int32, gate_ref.shape, 1)
    gate_ref[...] = jnp.where(lane < 2 * HEADS, w_ref[:, g0:g0 + HEAD_DIM], 0.0).astype(BF16)


def _regroup_w_in(w_in, layer):
    _, D, n_in = w_in.shape
    n_main = n_in - 2 * HEADS
    rows = REGROUP_ROWS
    return pl.pallas_call(
        _regroup_kernel,
        grid=(D // rows,),
        in_specs=[pl.BlockSpec((None, rows, n_in), lambda r: (layer, r, 0))],
        out_specs=[pl.BlockSpec((rows, n_main), lambda r: (r, 0)),
                   pl.BlockSpec((rows, HEAD_DIM), lambda r: (r, 0))],
        out_shape=[jax.ShapeDtypeStruct((D, n_main), BF16), jax.ShapeDtypeStruct((D, HEAD_DIM), BF16)],
        compiler_params=pltpu.CompilerParams(dimension_semantics=("arbitrary",)),
        name="regroup_w_in",
    )(w_in)


def _prepare(layer, norm1, w_in_all, b_in, ml_fgate_bias, hg_lb_logits, hg_norm, ml_norm, w_branch, w_out,
             norm2, w_up, ffn_conv_w, ffn_conv_b, w_down, final_norm):
    g0 = 8 * WIDTH
    g1 = g0 + 2 * HEADS
    w_main, w_gate = _regroup_w_in(w_in_all, layer)
    b_main = jnp.concatenate([b_in[:g0], b_in[g1:]])[None, :]
    gate_rows = BF16_SUBLANES
    w_gate_t = w_gate[:, :gate_rows].T
    b_gate = b_in[g0:g1] + jnp.concatenate([jnp.zeros((HEADS,), F32), ml_fgate_bias])
    return dict(
        norm1=norm1[None, :], w_main=w_main, b_main=b_main, w_gate=w_gate, w_gate_t=w_gate_t,
        b_gate_row=jnp.zeros((1, HEAD_DIM), F32).at[0, :2 * HEADS].set(b_gate),
        b_gate_col=jnp.zeros((gate_rows, 1), F32).at[:2 * HEADS, 0].set(b_gate),
        lb_logits=hg_lb_logits, hg_norm=hg_norm[None, :], ml_norm=ml_norm[None, :],
        w_branch=w_branch.astype(BF16), w_out=w_out.astype(BF16),
        norm2=norm2[None, :], w_up=w_up.astype(BF16), conv_w=ffn_conv_w, conv_b=ffn_conv_b[None, :],
        w_down=w_down.astype(BF16), final_norm=final_norm[None, :])


def _layer(x, s_hg, s_c, s_n, s_m, s_cv, mk, mv, p):
    h1, hg, c, n, m = _mixer(x, s_hg, s_c, s_n, s_m, mk, mv, p)
    y, cv = _ffn(h1, s_cv, p)
    return y, hg, c, n, m[:, :, 0], cv


def kernel(x_prompt, x_sample, state_hgrn, state_mlstm_C, state_mlstm_n, state_mlstm_m, state_ffn_conv, cache_mem_k, cache_mem_v, mem_prompt, norm1, w_in, b_in, ml_fgate_bias, hg_lb_logits, hg_norm, ml_norm, mem_norm, w_mem_kv, w_branch, w_out, norm2, w_up, ffn_conv_w, ffn_conv_b, w_down, final_norm):
    depth = norm1.shape[0]
    assert depth == 1, "single-layer encoder"
    assert hg_lb_logits.shape[0] == depth + 1
    Bp = x_prompt.shape[0]
    Bs = x_sample.shape[0]
    M = mem_prompt.shape[1]
    d_ff = w_down.shape[1]
    l = 0
    p = _prepare(l, norm1[l], w_in, b_in[l], ml_fgate_bias[l], hg_lb_logits, hg_norm[l], ml_norm[l],
                 w_branch[l], w_out[l], norm2[l], w_up[l], ffn_conv_w[l], ffn_conv_b[l], w_down[l], final_norm)

    mk_p, mv_p, mk_pb, mv_pb = _memory_kv(mem_prompt, mem_norm[l], w_mem_kv[l])
    zeros = lambda *s: jnp.zeros(s, F32)
    yp, hg_p, c_p, n_p, m_p, cv_p = _layer(
        x_prompt, zeros(Bp, HEADS, HEAD_DIM, HEAD_DIM), zeros(Bp, HEADS, HEAD_DIM, HEAD_DIM),
        zeros(Bp, HEADS, HEAD_DIM), zeros(Bp, HEADS), zeros(Bp, CONV_W - 1, d_ff), mk_pb, mv_pb, p)
    ys, hg_s, c_s, n_s, m_s, cv_s = _layer(
        x_sample, state_hgrn[l], state_mlstm_C[l], state_mlstm_n[l], state_mlstm_m[l], state_ffn_conv[l],
        cache_mem_k[l].reshape(Bs, M, WIDTH).astype(BF16), cache_mem_v[l].reshape(Bs, M, WIDTH).astype(BF16), p)

    return (yp, ys,
            hg_p[None], c_p[None], n_p[None], m_p[None],
            mk_p[None], mv_p[None], cv_p[None],
            hg_s[None], c_s[None], n_s[None], m_s[None], cv_s[None])
```
